```python
import jax, jax.numpy as jnp
from jax import lax
import numpy as np

D_MODEL = 1024
BATCH = 8
SEQ = 4096
DEPTH = 4
DEC_BATCH = 32
DEC_SEQ = 64
PAST_LEN = 2048

CHUNK = 64
Q_BLOCK = 128
BRANCH_WIDTH = D_MODEL // 2
SB_HEAD_DIM = 64
SB_HEADS = BRANCH_WIDTH // SB_HEAD_DIM
SB_WIDTH = SB_HEADS * SB_HEAD_DIM
POOL_WINDOWS = (2, 4, 8, 16)
POOL_GROUPS = len(POOL_WINDOWS)
POOL_WIDTH = BRANCH_WIDTH
POOL_GROUP_DIM = POOL_WIDTH // POOL_GROUPS
POOL_STATE = max(POOL_WINDOWS) - 1
GLA_HEADS = 4
GLA_KW = D_MODEL // 4
GLA_VW = BRANCH_WIDTH
GLA_DK = GLA_KW // GLA_HEADS
GLA_DV = GLA_VW // GLA_HEADS
GLA_GATE_RANK = 16
GLA_GATE_TAU = 16.0
N_BRANCH = 3
D_FF = ((8 * D_MODEL // 3 + 127) // 128) * 128
CONV_W = 3
NORM_EPS = 1e-6
IN_WIDTHS = (SB_WIDTH, SB_WIDTH, SB_WIDTH, POOL_WIDTH, GLA_KW, GLA_KW, GLA_VW, GLA_VW,
             GLA_GATE_RANK, N_BRANCH * D_MODEL)
D_IN = 3 * SB_WIDTH + POOL_WIDTH + 2 * GLA_KW + 2 * GLA_VW + GLA_GATE_RANK + N_BRANCH * D_MODEL

kernel_name = "hybrid_stickbreak_pool_gla_streaming_step"


def rmsnorm(x, g):
    xf = x.astype(jnp.float32)
    y = xf * lax.rsqrt(jnp.mean(xf * xf, axis=-1, keepdims=True) + NORM_EPS)
    return (y * g.astype(jnp.float32)).astype(x.dtype)


def split_cols(z, widths):
    out, o = [], 0
    for w in widths:
        out.append(z[..., o:o + w])
        o += w
    return out


def stick_breaking_block(q, k, v, q_pos, k_pos):
    z = jnp.einsum('bqhd,bkhd->bhqk', q.astype(jnp.float32), k.astype(jnp.float32)) * (SB_HEAD_DIM ** -0.5)
    mask = k_pos[None, :] < q_pos[:, None]
    log_stay = jnp.where(mask, jax.nn.log_sigmoid(-z), 0.0)
    later = lax.cumsum(log_stay, axis=3, reverse=True) - log_stay
    w = jnp.where(mask, jnp.exp(jax.nn.log_sigmoid(z) + later), 0.0)
    return jnp.einsum('bhqk,bkhd->bqhd', w, v.astype(jnp.float32)).astype(v.dtype)


def stick_breaking_prompt(q, k, v):
    B, T, H, dh = q.shape
    nb = T // Q_BLOCK
    qb = q.reshape(B, nb, Q_BLOCK, H, dh).transpose(1, 0, 2, 3, 4)
    pos = jnp.arange(T, dtype=jnp.int32)
    pb = pos.reshape(nb, Q_BLOCK)
    ob = lax.map(lambda a: stick_breaking_block(a[0], k, v, a[1], pos), (qb, pb))
    return ob.transpose(1, 0, 2, 3, 4).reshape(B, T, H, dh)


def multiscale_pool(u, prev, start):
    B, T, C = u.shape
    ext = jnp.concatenate([prev, u], axis=1).astype(jnp.float32)
    csum = jnp.concatenate([jnp.zeros((B, 1, C), jnp.float32), jnp.cumsum(ext, axis=1)], axis=1)
    end = csum[:, POOL_STATE + 1:]
    pos = start + jnp.arange(T, dtype=jnp.int32)
    outs = []
    for g, w in enumerate(POOL_WINDOWS):
        sl = slice(g * POOL_GROUP_DIM, (g + 1) * POOL_GROUP_DIM)
        s = end[..., sl] - csum[:, POOL_STATE + 1 - w: POOL_STATE + 1 - w + T, sl]
        cnt = jnp.minimum(pos + 1, w).astype(jnp.float32)[None, :, None]
        outs.append(s / cnt)
    mean = jnp.concatenate(outs, axis=-1)
    out = mean - u.astype(jnp.float32)
    return out.astype(u.dtype), ext[:, -POOL_STATE:].astype(u.dtype)


def gla_chunked(q, k, v, log_a, s0):
    B, T, H, dk = q.shape
    dv = v.shape[-1]
    c = min(CHUNK, T)
    n = T // c

    def blocks(a):
        return a.astype(jnp.float32).reshape(B, n, c, H, a.shape[-1]).transpose(1, 0, 3, 2, 4)

    causal = jnp.tril(jnp.ones((c, c), dtype=bool))

    def step(S, blk):
        qc, kc, vc, lc = blk
        b = jnp.cumsum(lc, axis=2)
        ref = b[:, :, c // 2:c // 2 + 1]
        att = jnp.einsum('bhid,bhjd->bhij', qc * jnp.exp(b - ref), kc * jnp.exp(ref - b))
        att = jnp.where(causal, att, 0.0)
        o = jnp.einsum('bhij,bhje->bhie', att, vc) + jnp.einsum('bhid,bhde->bhie', qc * jnp.exp(b), S)
        last = b[:, :, -1:]
        S = jnp.exp(last[:, :, 0, :, None]) * S + jnp.einsum('bhjd,bhje->bhde', kc * jnp.exp(last - b), vc)
        return S, o

    S, o = lax.scan(step, s0.astype(jnp.float32), (blocks(q), blocks(k), blocks(v), blocks(log_a)))
    o = o.transpose(1, 0, 3, 2, 4).reshape(B, T, H, dv)
    return o.astype(v.dtype), S.astype(s0.dtype)


def causal_dwconv(u, prev, w, b):
    T = u.shape[1]
    ext = jnp.concatenate([prev, u], axis=1)
    out = b + ext[:, 0:T] * w[0]
    for i in range(1, CONV_W):
        out = out + ext[:, i:i + T] * w[i]
    return out, ext[:, -(CONV_W - 1):]


def layer(x, start, prev_k, prev_v, prev_pool, prev_gla, prev_conv,
          norm1_g, w_in, w_gla_gate, b_gla_gate, gla_norm_g, w_pool, pool_scale,
          w_branch, w_out, norm2_g, w_up, conv_w, conv_b, w_down):
    B, T, _ = x.shape
    h = rmsnorm(x, norm1_g)
    z = h @ w_in
    (q_sb, k_sb, v_sb, u_pool, q_g, k_g, v_g, r_g, g_lr, gate_logits) = split_cols(z, IN_WIDTHS)

    q_sb = q_sb.reshape(B, T, SB_HEADS, SB_HEAD_DIM)
    k_sb = k_sb.reshape(B, T, SB_HEADS, SB_HEAD_DIM)
    v_sb = v_sb.reshape(B, T, SB_HEADS, SB_HEAD_DIM)
    if prev_k is None:
        o_sb = stick_breaking_prompt(q_sb, k_sb, v_sb)
    else:
        P = prev_k.shape[1]
        k_all = jnp.concatenate([prev_k, k_sb], axis=1)
        v_all = jnp.concatenate([prev_v, v_sb], axis=1)
        q_pos = start + jnp.arange(T, dtype=jnp.int32)
        k_pos = jnp.arange(P + T, dtype=jnp.int32)
        o_sb = stick_breaking_block(q_sb, k_all, v_all, q_pos, k_pos)

    pooled, pool_state = multiscale_pool(u_pool, prev_pool, start)
    o_pool = jnp.einsum('btgc,gcd->btgd', pooled.reshape(B, T, POOL_GROUPS, POOL_GROUP_DIM),
                        w_pool).reshape(B, T, POOL_WIDTH) * pool_scale

    log_a = jax.nn.log_sigmoid((g_lr @ w_gla_gate + b_gla_gate).astype(jnp.float32)) / GLA_GATE_TAU
    o_g, gla_state = gla_chunked(q_g.reshape(B, T, GLA_HEADS, GLA_DK) * (GLA_DK ** -0.5),
                                 k_g.reshape(B, T, GLA_HEADS, GLA_DK),
                                 v_g.reshape(B, T, GLA_HEADS, GLA_DV),
                                 log_a.reshape(B, T, GLA_HEADS, GLA_DK), prev_gla)
    o_gla = rmsnorm(o_g, gla_norm_g).reshape(B, T, GLA_VW) * jax.nn.silu(r_g)

    branches = jnp.stack([o_sb.reshape(B, T, SB_WIDTH), o_pool, o_gla], axis=2)
    proj = jnp.einsum('btnc,ncd->btnd', branches, w_branch)
    gates = jax.nn.sigmoid(gate_logits.reshape(B, T, N_BRANCH, D_MODEL))
    x = x + jnp.sum(gates * proj, axis=2) @ w_out

    h2 = rmsnorm(x, norm2_g)
    up, conv_state = causal_dwconv(h2 @ w_up, prev_conv, conv_w, conv_b)
    a, bgate = up[..., :D_FF], up[..., D_FF:]
    x = x + (jax.nn.gelu(a) * bgate) @ w_down
    return x, (k_sb, v_sb, pool_state, gla_state, conv_state)


def setup_inputs(seed: int = 0) -> dict:
    key = jax.random.key(seed)
    ks = jax.random.split(key, 32)

    def nrm(k, shape, s):
        return jax.random.normal(k, shape, jnp.float32) * s

    return {
        "x_prompt": nrm(ks[0], (BATCH, SEQ, D_MODEL), 1.0),
        "x_sample": nrm(ks[1], (DEC_BATCH, DEC_SEQ, D_MODEL), 1.0),
        "cache_k": nrm(ks[2], (DEPTH, DEC_BATCH, PAST_LEN, SB_HEADS, SB_HEAD_DIM), 1.0),
        "cache_v": nrm(ks[3], (DEPTH, DEC_BATCH, PAST_LEN, SB_HEADS, SB_HEAD_DIM), 1.0),
        "state_pool": nrm(ks[4], (DEPTH, DEC_BATCH, POOL_STATE, POOL_WIDTH), 1.0),
        "state_gla": nrm(ks[5], (DEPTH, DEC_BATCH, GLA_HEADS, GLA_DK, GLA_DV), 2.0),
        "state_conv": nrm(ks[6], (DEPTH, DEC_BATCH, CONV_W - 1, 2 * D_FF), 1.0),
        "norm1_g": 1.0 + nrm(ks[7], (DEPTH, D_MODEL), 0.02),
        "w_in": nrm(ks[8], (DEPTH, D_MODEL, D_IN), D_MODEL ** -0.5),
        "w_gla_gate": nrm(ks[9], (DEPTH, GLA_GATE_RANK, GLA_KW), GLA_GATE_RANK ** -0.5),
        "b_gla_gate": nrm(ks[10], (DEPTH, GLA_KW), 0.1),
        "gla_norm_g": 1.0 + nrm(ks[11], (DEPTH, GLA_DV), 0.02),
        "w_pool": nrm(ks[12], (DEPTH, POOL_GROUPS, POOL_GROUP_DIM, POOL_GROUP_DIM), POOL_GROUP_DIM ** -0.5),
        "pool_scale": 1.0 + nrm(ks[13], (DEPTH, POOL_WIDTH), 0.02),
        "w_branch": nrm(ks[14], (DEPTH, N_BRANCH, BRANCH_WIDTH, D_MODEL), BRANCH_WIDTH ** -0.5),
        "w_out": nrm(ks[15], (DEPTH, D_MODEL, D_MODEL), D_MODEL ** -0.5),
        "norm2_g": 1.0 + nrm(ks[16], (DEPTH, D_MODEL), 0.02),
        "w_up": nrm(ks[17], (DEPTH, D_MODEL, 2 * D_FF), D_MODEL ** -0.5),
        "conv_w": nrm(ks[18], (DEPTH, CONV_W, 2 * D_FF), CONV_W ** -0.5),
        "conv_b": nrm(ks[19], (DEPTH, 2 * D_FF), 0.02),
        "w_down": nrm(ks[20], (DEPTH, D_FF, D_MODEL), D_FF ** -0.5),
        "final_g": 1.0 + nrm(ks[21], (D_MODEL,), 0.02),
    }


def reference(x_prompt, x_sample, cache_k, cache_v, state_pool, state_gla, state_conv,
              norm1_g, w_in, w_gla_gate, b_gla_gate, gla_norm_g, w_pool, pool_scale,
              w_branch, w_out, norm2_g, w_up, conv_w, conv_b, w_down, final_g):
    Bp = x_prompt.shape[0]
    dt = x_prompt.dtype
    past = cache_k.shape[2]
    xp, xs = x_prompt, x_sample
    kp, vp, pp, gp, cp = [], [], [], [], []
    ksm, vsm, psm, gsm, csm = [], [], [], [], []
    for l in range(DEPTH):
        w = (norm1_g[l], w_in[l], w_gla_gate[l], b_gla_gate[l], gla_norm_g[l], w_pool[l],
             pool_scale[l], w_branch[l], w_out[l], norm2_g[l], w_up[l], conv_w[l], conv_b[l], w_down[l])
        xp, st = layer(xp, 0, None, None,
                       jnp.zeros((Bp, POOL_STATE, POOL_WIDTH), dt),
                       jnp.zeros((Bp, GLA_HEADS, GLA_DK, GLA_DV), dt),
                       jnp.zeros((Bp, CONV_W - 1, 2 * D_FF), dt), *w)
        kp.append(st[0]); vp.append(st[1]); pp.append(st[2]); gp.append(st[3]); cp.append(st[4])
        xs, st = layer(xs, past, cache_k[l], cache_v[l], state_pool[l], state_gla[l], state_conv[l], *w)
        ksm.append(st[0]); vsm.append(st[1]); psm.append(st[2]); gsm.append(st[3]); csm.append(st[4])
    y_prompt = rmsnorm(xp, final_g)
    y_sample = rmsnorm(xs, final_g)
    return (y_prompt, y_sample,
            jnp.stack(kp), jnp.stack(vp), jnp.stack(pp), jnp.stack(gp), jnp.stack(cp),
            jnp.stack(ksm), jnp.stack(vsm), jnp.stack(psm), jnp.stack(gsm), jnp.stack(csm))
```

```python
import functools

import jax
import jax.numpy as jnp
from jax import lax
from jax.experimental import pallas as pl
from jax.experimental.pallas import tpu as pltpu

F32 = jnp.float32
BF16 = jnp.bfloat16

NORM_EPS = 1e-6
CHUNK = 64
SB_HEAD_DIM = 64
POOL_WINDOWS = (2, 4, 8, 16)
POOL_STATE = max(POOL_WINDOWS) - 1
POOL_PAD = 16
GLA_HEADS = 4
GLA_GATE_TAU = 16.0
CONV_W = 3
CONV_PAD = 8
LANES = 128
V7X_SCOPED_VMEM_BYTES = 60000 * 1024

ROW_TILE = 512
SB_BLOCK = 256


def _cparams(n_axes, vmem_bytes):
    return pltpu.CompilerParams(
        dimension_semantics=("arbitrary",) * n_axes,
        vmem_limit_bytes=int(min(vmem_bytes, V7X_SCOPED_VMEM_BYTES)),
    )


def _nbytes(shape, dtype):
    n = 1
    for s in shape:
        n *= s
    return n * jnp.dtype(dtype).itemsize


def _const_spec(shape):
    nd = len(shape)
    return pl.BlockSpec(shape, lambda *_: (0,) * nd, pipeline_mode=pl.Buffered(1))


def _rmsnorm(xf, g):
    ms = jnp.mean(xf * xf, axis=-1, keepdims=True)
    return (xf * lax.rsqrt(ms + NORM_EPS)) * g


def _log_sigmoid(x):
    return jnp.minimum(x, 0.0) - jnp.log1p(jnp.exp(-jnp.abs(x)))


def _split_bf16(x):
    hi = x.astype(BF16)
    lo = (x - hi.astype(F32)).astype(BF16)
    return hi, lo


def _dot(a, b):
    return jnp.dot(a, b, preferred_element_type=F32)


def _dot_nt(a, b):
    return lax.dot_general(a, b, (((1,), (1,)), ((), ())), preferred_element_type=F32)


def _dot_tn(a, b):
    return lax.dot_general(a, b, (((0,), (0,)), ((), ())), preferred_element_type=F32)


def _inproj_kernel(x_ref, g_ref, wa_ref, wl_ref, wg_ref, bg_ref,
                   sb_ref, k_ref, v_ref, u_ref, gf_ref, vg_ref, *, bw, kw):
    h = _rmsnorm(x_ref[...], g_ref[...]).astype(BF16)

    def proj(c0, c1):
        return _dot(h, wa_ref[:, c0:c1])

    sb_ref[:, 0:bw] = (proj(0, bw) * (SB_HEAD_DIM ** -0.5)).astype(BF16)
    k = proj(bw, 2 * bw)
    k_ref[...] = k
    sb_ref[:, bw:2 * bw] = k.astype(BF16)
    v = proj(2 * bw, 3 * bw)
    v_ref[...] = v
    sb_ref[:, 2 * bw:3 * bw] = v.astype(BF16)
    u_ref[...] = proj(3 * bw, 4 * bw)
    o = 4 * bw
    gf_ref[:, 0:kw] = proj(o, o + kw)
    gf_ref[:, kw:2 * kw] = proj(o + kw, o + 2 * kw)
    vg_ref[...] = proj(o + 2 * kw, o + 2 * kw + bw).astype(BF16)
    gf_ref[:, 3 * kw:3 * kw + bw] = proj(o + 2 * kw + bw, o + 2 * kw + 2 * bw)
    g_lr = _dot(h, wl_ref[...]).astype(BF16)
    pre = _dot(g_lr, wg_ref[...]) + bg_ref[...]
    gf_ref[:, 2 * kw:3 * kw] = _log_sigmoid(pre) * (1.0 / GLA_GATE_TAU)


def _inproj(x, g1, wa, wl, wg, bg, *, bw, kw):
    n, d = x.shape
    tm = min(ROW_TILE, n)
    gfw = 3 * kw + bw
    out_shape = (
        jax.ShapeDtypeStruct((n, 3 * bw), BF16),
        jax.ShapeDtypeStruct((n, bw), F32),
        jax.ShapeDtypeStruct((n, bw), F32),
        jax.ShapeDtypeStruct((n, bw), F32),
        jax.ShapeDtypeStruct((n, gfw), F32),
        jax.ShapeDtypeStruct((n, bw), BF16),
    )
    row = lambda w: pl.BlockSpec((tm, w), lambda i: (i, 0))
    blocks = 2 * (_nbytes((tm, d), F32) + _nbytes((tm, 3 * bw), BF16) + 3 * _nbytes((tm, bw), F32)
                  + _nbytes((tm, gfw), F32) + _nbytes((tm, bw), BF16))
    weights = _nbytes(wa.shape, BF16) + _nbytes(wl.shape, BF16) + _nbytes(wg.shape, BF16)
    temps = 4 * _nbytes((tm, d), F32)
    return pl.pallas_call(
        functools.partial(_inproj_kernel, bw=bw, kw=kw),
        out_shape=out_shape,
        grid=(n // tm,),
        in_specs=[row(d), _const_spec(g1.shape), _const_spec(wa.shape), _const_spec(wl.shape),
                  _const_spec(wg.shape), _const_spec(bg.shape)],
        out_specs=(row(3 * bw), row(bw), row(bw), row(bw), row(gfw), row(bw)),
        compiler_params=_cparams(1, blocks + weights + temps),
        name="inproj",
    )(x, g1, wa, wl, wg, bg)


def _strict_upper_ones(n):
    r = lax.broadcasted_iota(jnp.int32, (n, n), 0)
    c = lax.broadcasted_iota(jnp.int32, (n, n), 1)
    return jnp.where(r > c, 1.0, 0.0).astype(BF16)


def _sb_block(qh, kb, vb, u, carry, mask):
    s = _dot_nt(qh, kb)
    sp = jnp.log1p(jnp.exp(-jnp.abs(s)))
    ls_pos = jnp.minimum(s, 0.0) - sp
    ls_neg = jnp.minimum(-s, 0.0) - sp
    if mask is not None:
        ls_neg = jnp.where(mask, ls_neg, 0.0)
    hi, lo = _split_bf16(ls_neg)
    later = _dot(hi, u) + _dot(lo, u)
    w = jnp.exp(ls_pos + later + carry)
    if mask is not None:
        w = jnp.where(mask, w, 0.0)
    pv = _dot(w.astype(BF16), vb)
    return pv, carry + jnp.sum(ls_neg, axis=-1, keepdims=True)


def _head_masks():
    lane = lax.broadcasted_iota(jnp.int32, (1, LANES), 1)
    return lane < SB_HEAD_DIM


def _sb_prompt_kernel(q_ref, k_ref, v_ref, o_ref, acc_ref, *, blk):
    i = pl.program_id(2)
    first = _head_masks()
    u = _strict_upper_ones(blk)
    r = lax.broadcasted_iota(jnp.int32, (blk, blk), 0)
    c = lax.broadcasted_iota(jnp.int32, (blk, blk), 1)
    causal = c < r
    q = q_ref[0]
    qs = (jnp.where(first, q, jnp.zeros_like(q)), jnp.where(first, jnp.zeros_like(q), q))

    off = pl.multiple_of(i * blk, blk)
    kb = k_ref[0, pl.ds(off, blk), :]
    vb = v_ref[0, pl.ds(off, blk), :]
    carries = []
    for hh in range(2):
        pv, cr = _sb_block(qs[hh], kb, vb, u, jnp.zeros((blk, 1), F32), causal)
        acc_ref[hh] = pv
        carries.append(cr)

    def body(t, carry):
        off = pl.multiple_of((i - 1 - t) * blk, blk)
        kb = k_ref[0, pl.ds(off, blk), :]
        vb = v_ref[0, pl.ds(off, blk), :]
        out = []
        for hh in range(2):
            pv, cr = _sb_block(qs[hh], kb, vb, u, carry[hh], None)
            acc_ref[hh] += pv
            out.append(cr)
        return tuple(out)

    lax.fori_loop(0, i, body, tuple(carries))
    o_ref[0] = jnp.where(first, acc_ref[0], acc_ref[1]).astype(o_ref.dtype)


def _sb_prompt(sb, *, batch, seq, bw):
    blk = min(SB_BLOCK, seq)
    nq = seq // blk
    npair = bw // LANES
    sb3 = sb.reshape(batch, seq, 3 * bw)
    vmem = (2 * (2 * _nbytes((blk, LANES), BF16) + 2 * _nbytes((seq, LANES), BF16))
            + 2 * _nbytes((blk, LANES), F32) + 16 * _nbytes((blk, blk), F32))
    out = pl.pallas_call(
        functools.partial(_sb_prompt_kernel, blk=blk),
        out_shape=jax.ShapeDtypeStruct((batch, seq, bw), BF16),
        grid=(batch, npair, nq),
        in_specs=[
            pl.BlockSpec((1, blk, LANES), lambda b, p, i: (b, i, p)),
            pl.BlockSpec((1, seq, LANES), lambda b, p, i: (b, 0, npair + p)),
            pl.BlockSpec((1, seq, LANES), lambda b, p, i: (b, 0, 2 * npair + p)),
        ],
        out_specs=pl.BlockSpec((1, blk, LANES), lambda b, p, i: (b, i, p)),
        scratch_shapes=[pltpu.VMEM((2, blk, LANES), F32)],
        compiler_params=_cparams(3, vmem),
        name="sb_prompt",
    )(sb3, sb3, sb3)
    return out.reshape(batch * seq, bw)


def _sb_sample_kernel(q_ref, kn_ref, vn_ref, kp_ref, vp_ref, o_ref, acc_ref, *, tq, blk, npast):
    first = _head_masks()
    u_new = _strict_upper_ones(tq)
    u = _strict_upper_ones(blk)
    r = lax.broadcasted_iota(jnp.int32, (tq, tq), 0)
    c = lax.broadcasted_iota(jnp.int32, (tq, tq), 1)
    causal = c < r
    q = q_ref[0]
    qs = (jnp.where(first, q, jnp.zeros_like(q)), jnp.where(first, jnp.zeros_like(q), q))
    carries = []
    for hh in range(2):
        pv, cr = _sb_block(qs[hh], kn_ref[0], vn_ref[0], u_new, jnp.zeros((tq, 1), F32), causal)
        acc_ref[hh] = pv
        carries.append(cr)

    def body(t, carry):
        off = pl.multiple_of((npast - 1 - t) * blk, blk)
        kb = kp_ref[0, pl.ds(off, blk), :].astype(BF16)
        vb = vp_ref[0, pl.ds(off, blk), :].astype(BF16)
        out = []
        for hh in range(2):
            pv, cr = _sb_block(qs[hh], kb, vb, u, carry[hh], None)
            acc_ref[hh] += pv
            out.append(cr)
        return tuple(out)

    lax.fori_loop(0, npast, body, tuple(carries))
    o_ref[0] = jnp.where(first, acc_ref[0], acc_ref[1]).astype(o_ref.dtype)


def _sb_sample(sb, past_k, past_v, *, batch, seq, bw):
    past = past_k.shape[1]
    blk = min(SB_BLOCK, past)
    npair = bw // LANES
    sb3 = sb.reshape(batch, seq, 3 * bw)
    vmem = (2 * (4 * _nbytes((seq, LANES), BF16) + 2 * _nbytes((past, LANES), F32))
            + 2 * _nbytes((seq, LANES), F32) + 16 * _nbytes((seq, blk), F32))
    new = lambda col0: pl.BlockSpec((1, seq, LANES), lambda b, p: (b, 0, col0 + p))
    old = pl.BlockSpec((1, past, LANES), lambda b, p: (b, 0, p))
    out = pl.pallas_call(
        functools.partial(_sb_sample_kernel, tq=seq, blk=blk, npast=past // blk),
        out_shape=jax.ShapeDtypeStruct((batch, seq, bw), BF16),
        grid=(batch, npair),
        in_specs=[new(0), new(npair), new(2 * npair), old, old],
        out_specs=new(0),
        scratch_shapes=[pltpu.VMEM((2, seq, LANES), F32)],
        compiler_params=_cparams(2, vmem),
        name="sb_sample",
    )(sb3, sb3, sb3, past_k, past_v)
    return out.reshape(batch * seq, bw)


def _pool_kernel(*refs, tq, gdim, start, has_prev):
    if has_prev:
        u_ref, prev_ref, w_ref, sc_ref, o_ref, st_ref, ext_ref = refs
    else:
        u_ref, w_ref, sc_ref, o_ref, st_ref, ext_ref = refs
    t = pl.program_id(1)

    @pl.when(t == 0)
    def _():
        ext_ref[0:POOL_PAD, :] = jnp.zeros((POOL_PAD, ext_ref.shape[1]), F32)
        if has_prev:
            ext_ref[POOL_PAD - POOL_STATE:POOL_PAD, :] = prev_ref[0]

    @pl.when(t > 0)
    def _():
        ext_ref[0:POOL_PAD, :] = ext_ref[tq:tq + POOL_PAD, :]

    ext_ref[POOL_PAD:POOL_PAD + tq, :] = u_ref[0]
    pos = start + t * tq + lax.broadcasted_iota(jnp.int32, (tq, 1), 0)
    for g, w in enumerate(POOL_WINDOWS):
        cols = slice(g * gdim, (g + 1) * gdim)
        tok = ext_ref[POOL_PAD:POOL_PAD + tq, cols]
        s = tok
        for d in range(1, w):
            s = s + ext_ref[POOL_PAD - d:POOL_PAD - d + tq, cols]
        cnt = jnp.minimum(pos + 1, w).astype(F32)
        pooled = (s / cnt - tok).astype(BF16)
        o_ref[0, :, cols] = (_dot(pooled, w_ref[g]) * sc_ref[:, cols]).astype(o_ref.dtype)

    @pl.when(t == pl.num_programs(1) - 1)
    def _():
        st_ref[0] = ext_ref[POOL_PAD + tq - POOL_STATE:POOL_PAD + tq, :]


def _pool(u, prev, w_pool, scale, *, batch, seq, start):
    bw = u.shape[-1]
    tq = min(ROW_TILE, seq)
    gdim = bw // len(POOL_WINDOWS)
    has_prev = prev is not None
    u3 = u.reshape(batch, seq, bw)
    in_specs = [pl.BlockSpec((1, tq, bw), lambda b, t: (b, t, 0))]
    args = [u3]
    if has_prev:
        in_specs.append(pl.BlockSpec((1, POOL_STATE, bw), lambda b, t: (b, 0, 0)))
        args.append(prev)
    in_specs += [_const_spec(w_pool.shape), _const_spec(scale.shape)]
    args += [w_pool, scale]
    vmem = (2 * (_nbytes((tq, bw), F32) + _nbytes((tq, bw), BF16) + 2 * _nbytes((POOL_PAD, bw), F32))
            + _nbytes((POOL_PAD + tq, bw), F32) + _nbytes(w_pool.shape, BF16) + 6 * _nbytes((tq, bw), F32))
    o, st = pl.pallas_call(
        functools.partial(_pool_kernel, tq=tq, gdim=gdim, start=start, has_prev=has_prev),
        out_shape=(jax.ShapeDtypeStruct((batch, seq, bw), BF16),
                   jax.ShapeDtypeStruct((batch, POOL_STATE, bw), F32)),
        grid=(batch, seq // tq),
        in_specs=in_specs,
        out_specs=(pl.BlockSpec((1, tq, bw), lambda b, t: (b, t, 0)),
                   pl.BlockSpec((1, POOL_STATE, bw), lambda b, t: (b, 0, 0))),
        scratch_shapes=[pltpu.VMEM((POOL_PAD + tq, bw), F32)],
        compiler_params=_cparams(2, vmem),
        name="pool",
    )(*args)
    return o.reshape(batch * seq, bw), st


def _gla_kernel(*refs, tt, kw, vw, has_state):
    if has_state:
        gf_ref, vg_ref, s0_ref, gn_ref, o_ref, sT_out_ref, sT_ref, og_ref = refs
    else:
        gf_ref, vg_ref, gn_ref, o_ref, sT_out_ref, sT_ref, og_ref = refs
    t = pl.program_id(1)
    dk = kw // GLA_HEADS
    dv = vw // GLA_HEADS

    @pl.when(t == 0)
    def _():
        if has_state:
            sT_ref[...] = s0_ref[0]
        else:
            sT_ref[...] = jnp.zeros(sT_ref.shape, F32)

    def block_mask(shape, rdiv, cdiv):
        r = lax.broadcasted_iota(jnp.int32, shape, 0) // rdiv
        c = lax.broadcasted_iota(jnp.int32, shape, 1) // cdiv
        return r == c

    r = lax.broadcasted_iota(jnp.int32, (CHUNK, CHUNK), 0)
    c = lax.broadcasted_iota(jnp.int32, (CHUNK, CHUNK), 1)
    tril = jnp.where(r >= c, 1.0, 0.0).astype(BF16)
    m_kk = block_mask((GLA_HEADS * CHUNK, kw), CHUNK, dk)
    m_vv = block_mask((GLA_HEADS * CHUNK, vw), CHUNK, dv)
    m_st = block_mask((vw, kw), dv, dk)
    qi = lax.broadcasted_iota(jnp.int32, (CHUNK, GLA_HEADS * CHUNK), 0)
    kj = lax.broadcasted_iota(jnp.int32, (CHUNK, GLA_HEADS * CHUNK), 1) % CHUNK
    causal = qi >= kj

    for ci in range(tt // CHUNK):
        rows = slice(ci * CHUNK, (ci + 1) * CHUNK)
        q = gf_ref[0, rows, 0:kw] * (dk ** -0.5)
        k = gf_ref[0, rows, kw:2 * kw]
        la = gf_ref[0, rows, 2 * kw:3 * kw]
        v = vg_ref[0, rows, :]
        hi, lo = _split_bf16(la)
        b = _dot(tril, hi) + _dot(tril, lo)
        bref = b[CHUNK // 2:CHUNK // 2 + 1, :]
        blast = b[CHUNK - 1:CHUNK, :]
        qe = (q * jnp.exp(b - bref)).astype(BF16)
        ke = (k * jnp.exp(bref - b)).astype(BF16)
        qb = (q * jnp.exp(b)).astype(BF16)
        kl = (k * jnp.exp(blast - b)).astype(BF16)
        ke_bd = jnp.where(m_kk, jnp.concatenate([ke] * GLA_HEADS, axis=0), jnp.zeros((), BF16))
        att = jnp.where(causal, _dot_nt(qe, ke_bd), 0.0).astype(BF16)
        v_bd = jnp.where(m_vv, jnp.concatenate([v] * GLA_HEADS, axis=0), jnp.zeros((), BF16))
        s_t = sT_ref[...]
        og_ref[rows, :] = _dot(att, v_bd) + _dot_nt(qb, s_t.astype(BF16))
        upd = jnp.where(m_st, _dot_tn(v, kl), 0.0)
        sT_ref[...] = s_t * jnp.exp(blast) + upd

    gn = gn_ref[...]
    for h in range(GLA_HEADS):
        cols = slice(h * dv, (h + 1) * dv)
        y = _rmsnorm(og_ref[:, cols], gn)
        rg = gf_ref[0, :, 3 * kw + h * dv:3 * kw + (h + 1) * dv]
        o_ref[0, :, cols] = (y * (rg / (1.0 + jnp.exp(-rg)))).astype(o_ref.dtype)

    @pl.when(t == pl.num_programs(1) - 1)
    def _():
        sT_out_ref[0] = sT_ref[...]


def _gla(gf, vg, s0, gnorm, *, batch, seq, kw, vw):
    tt = min(ROW_TILE, seq)
    dk, dv = kw // GLA_HEADS, vw // GLA_HEADS
    gfw = gf.shape[-1]
    has_state = s0 is not None
    in_specs = [pl.BlockSpec((1, tt, gfw), lambda b, t: (b, t, 0)),
                pl.BlockSpec((1, tt, vw), lambda b, t: (b, t, 0))]
    args = [gf.reshape(batch, seq, gfw), vg.reshape(batch, seq, vw)]
    if has_state:
        s0t = jnp.zeros((batch, GLA_HEADS, dv, GLA_HEADS, dk), F32)
        for h in range(GLA_HEADS):
            s0t = s0t.at[:, h, :, h, :].set(jnp.swapaxes(s0[:, h], 1, 2))
        s0t = s0t.reshape(batch, vw, kw)
        in_specs.append(pl.BlockSpec((1, vw, kw), lambda b, t: (b, 0, 0)))
        args.append(s0t)
    in_specs.append(_const_spec(gnorm.shape))
    args.append(gnorm)
    vmem = (2 * (_nbytes((tt, gfw), F32) + 2 * _nbytes((tt, vw), BF16) + 2 * _nbytes((vw, kw), F32))
            + _nbytes((vw, kw), F32) + _nbytes((tt, vw), F32) + 24 * _nbytes((GLA_HEADS * CHUNK, vw), F32))
    o, st = pl.pallas_call(
        functools.partial(_gla_kernel, tt=tt, kw=kw, vw=vw, has_state=has_state),
        out_shape=(jax.ShapeDtypeStruct((batch, seq, vw), BF16),
                   jax.ShapeDtypeStruct((batch, vw, kw), F32)),
        grid=(batch, seq // tt),
        in_specs=in_specs,
        out_specs=(pl.BlockSpec((1, tt, vw), lambda b, t: (b, t, 0)),
                   pl.BlockSpec((1, vw, kw), lambda b, t: (b, 0, 0))),
        scratch_shapes=[pltpu.VMEM((vw, kw), F32), pltpu.VMEM((tt, vw), F32)],
        compiler_params=_cparams(2, vmem),
        name="gla",
    )(*args)
    st5 = st.reshape(batch, GLA_HEADS, dv, GLA_HEADS, dk)
    state = jnp.stack([st5[:, h, :, h, :] for h in range(GLA_HEADS)], axis=1)
    return o.reshape(batch * seq, vw), jnp.swapaxes(state, 2, 3)


def _merge_kernel(x_ref, g_ref, osb_ref, opool_ref, ogla_ref, wgate_ref, wbr_ref, wout_ref, o_ref, *, d):
    xf = x_ref[...]
    h = _rmsnorm(xf, g_ref[...]).astype(BF16)
    m = None
    for n, br_ref in enumerate((osb_ref, opool_ref, ogla_ref)):
        logits = _dot(h, wgate_ref[:, n * d:(n + 1) * d])
        gate = 1.0 / (1.0 + jnp.exp(-logits))
        term = gate * _dot(br_ref[...], wbr_ref[n])
        m = term if m is None else m + term
    o_ref[...] = xf + _dot(m.astype(BF16), wout_ref[...])


def _merge(x, g1, o_sb, o_pool, o_gla, wgate, wbr, wout):
    n, d = x.shape
    bw = o_sb.shape[-1]
    tm = min(ROW_TILE, n)
    row = lambda w: pl.BlockSpec((tm, w), lambda i: (i, 0))
    vmem = (2 * (2 * _nbytes((tm, d), F32) + 3 * _nbytes((tm, bw), BF16))
            + _nbytes(wgate.shape, BF16) + _nbytes(wbr.shape, BF16) + _nbytes(wout.shape, BF16)
            + 6 * _nbytes((tm, d), F32))
    return pl.pallas_call(
        functools.partial(_merge_kernel, d=d),
        out_shape=jax.ShapeDtypeStruct((n, d), F32),
        grid=(n // tm,),
        in_specs=[row(d), _const_spec(g1.shape), row(bw), row(bw), row(bw),
                  _const_spec(wgate.shape), _const_spec(wbr.shape), _const_spec(wout.shape)],
        out_specs=row(d),
        compiler_params=_cparams(1, vmem),
        name="merge",
    )(x, g1, o_sb, o_pool, o_gla, wgate, wbr, wout)


FFN_COLS = 256


def _gelu_tanh(x):
    c = 0.7978845608028654
    return 0.5 * x * (1.0 + jnp.tanh(c * (x + 0.044715 * (x * x * x))))


def _ffn_kernel(*refs, nseg, seg, dff, has_prev):
    if has_prev:
        x_ref, g_ref, prev_ref, wup_ref, cw_ref, cb_ref, wdn_ref, o_ref, st_ref, ext_ref, gact_ref, tail_ref = refs
    else:
        x_ref, g_ref, wup_ref, cw_ref, cb_ref, wdn_ref, o_ref, st_ref, ext_ref, gact_ref, tail_ref = refs
    t = pl.program_id(1)
    xf = x_ref[...]
    h = _rmsnorm(xf, g_ref[...]).astype(BF16)
    pad0 = CONV_PAD - (CONV_W - 1)

    if not has_prev:
        @pl.when(t == 0)
        def _():
            tail_ref[...] = jnp.zeros(tail_ref.shape, F32)

    def conv_half(c0):
        cols = slice(c0, c0 + FFN_COLS)
        up = _dot(h, wup_ref[:, cols])
        outs = []
        for s in range(nseg):
            if has_prev:
                ext_ref[s, pad0:CONV_PAD, :] = prev_ref[s, :, cols]
            else:
                ext_ref[s, 0:CONV_PAD, :] = tail_ref[:, cols]
            ext_ref[s, CONV_PAD:CONV_PAD + seg, :] = up[s * seg:(s + 1) * seg, :]
            acc = cb_ref[:, cols] + ext_ref[s, pad0:pad0 + seg, :] * cw_ref[0:1, cols]
            for i in range(1, CONV_W):
                acc = acc + ext_ref[s, pad0 + i:pad0 + i + seg, :] * cw_ref[i:i + 1, cols]
            outs.append(acc)
            if has_prev:
                st_ref[s, :, cols] = ext_ref[s, CONV_PAD + seg - (CONV_W - 1):CONV_PAD + seg, :]
            else:
                tail_ref[:, cols] = ext_ref[s, seg:seg + CONV_PAD, :]
        return outs[0] if nseg == 1 else jnp.concatenate(outs, axis=0)

    for c0 in range(0, dff, FFN_COLS):
        a = conv_half(c0)
        b = conv_half(dff + c0)
        gact_ref[:, c0:c0 + FFN_COLS] = (_gelu_tanh(a) * b).astype(BF16)

    o_ref[...] = xf + _dot(gact_ref[...], wdn_ref[...])

    if not has_prev:
        @pl.when(t == pl.num_programs(1) - 1)
        def _():
            st_ref[0] = tail_ref[CONV_PAD - (CONV_W - 1):CONV_PAD, :]


def _ffn(x, g2, prev, wup, cw, cb, wdn, *, batch, seq):
    n, d = x.shape
    dff = wdn.shape[0]
    has_prev = prev is not None
    if has_prev:
        nseg = max(1, min(ROW_TILE // seq, batch))
        seg = seq
        grid = (batch // nseg, 1)
        xmap = lambda i, t: (i, 0)
        stmap = lambda i, t: (i, 0, 0)
    else:
        nseg = 1
        seg = min(ROW_TILE, seq)
        grid = (batch, seq // seg)
        nt = seq // seg
        xmap = lambda b, t: (b * nt + t, 0)
        stmap = lambda b, t: (b, 0, 0)
    tm = nseg * seg
    in_specs = [pl.BlockSpec((tm, d), xmap), _const_spec(g2.shape)]
    args = [x, g2]
    if has_prev:
        in_specs.append(pl.BlockSpec((nseg, CONV_W - 1, 2 * dff), stmap))
        args.append(prev)
    in_specs += [_const_spec(wup.shape), _const_spec(cw.shape), _const_spec(cb.shape), _const_spec(wdn.shape)]
    args += [wup, cw, cb, wdn]
    vmem = (2 * (2 * _nbytes((tm, d), F32) + 2 * _nbytes((nseg, CONV_W - 1, 2 * dff), F32))
            + _nbytes(wup.shape, BF16) + _nbytes(wdn.shape, BF16) + 2 * _nbytes(cw.shape, F32)
            + _nbytes((nseg, CONV_PAD + seg, FFN_COLS), F32) + _nbytes((tm, dff), BF16)
            + _nbytes((CONV_PAD, 2 * dff), F32) + 4 * _nbytes((tm, d), F32) + 12 * _nbytes((tm, FFN_COLS), F32))
    o, st = pl.pallas_call(
        functools.partial(_ffn_kernel, nseg=nseg, seg=seg, dff=dff, has_prev=has_prev),
        out_shape=(jax.ShapeDtypeStruct((n, d), F32),
                   jax.ShapeDtypeStruct((batch, CONV_W - 1, 2 * dff), F32)),
        grid=grid,
        in_specs=in_specs,
        out_specs=(pl.BlockSpec((tm, d), xmap), pl.BlockSpec((nseg, CONV_W - 1, 2 * dff), stmap)),
        scratch_shapes=[pltpu.VMEM((nseg, CONV_PAD + seg, FFN_COLS), F32),
                        pltpu.VMEM((tm, dff), BF16),
                        pltpu.VMEM((CONV_PAD, 2 * dff), F32)],
        compiler_params=_cparams(2, vmem),
        name="ffn",
    )(*args)
    return o, st


def _final_norm_kernel(x_ref, g_ref, o_ref):
    o_ref[...] = _rmsnorm(x_ref[...], g_ref[...])


def _final_norm(x, g):
    n, d = x.shape
    tm = min(ROW_TILE, n)
    return pl.pallas_call(
        _final_norm_kernel,
        out_shape=jax.ShapeDtypeStruct((n, d), F32),
        grid=(n // tm,),
        in_specs=[pl.BlockSpec((tm, d), lambda i: (i, 0)), _const_spec(g.shape)],
        out_specs=pl.BlockSpec((tm, d), lambda i: (i, 0)),
        compiler_params=_cparams(1, 6 * _nbytes((tm, d), F32)),
        name="final_norm",
    )(x, g)


def _layer(x, w, *, batch, seq, start, past_k, past_v, prev_pool, prev_gla, prev_conv):
    d = x.shape[-1]
    bw, kw = d // 2, d // 4
    sb, k_f, v_f, u, gf, vg = _inproj(x, w["g1"], w["wa"], w["wl"], w["wg"], w["bg"], bw=bw, kw=kw)
    if past_k is None:
        o_sb = _sb_prompt(sb, batch=batch, seq=seq, bw=bw)
    else:
        o_sb = _sb_sample(sb, past_k, past_v, batch=batch, seq=seq, bw=bw)
    o_pool, pool_state = _pool(u, prev_pool, w["w_pool"], w["pool_scale"], batch=batch, seq=seq, start=start)
    o_gla, gla_state = _gla(gf, vg, prev_gla, w["gnorm"], batch=batch, seq=seq, kw=kw, vw=bw)
    x = _merge(x, w["g1"], o_sb, o_pool, o_gla, w["wgate"], w["wbr"], w["wout"])
    x, conv_state = _ffn(x, w["g2"], prev_conv, w["wup"], w["cw"], w["cb"], w["wdn"], batch=batch, seq=seq)
    heads = bw // SB_HEAD_DIM
    kv_shape = (batch, seq, heads, SB_HEAD_DIM)
    return x, (k_f.reshape(kv_shape), v_f.reshape(kv_shape), pool_state, gla_state, conv_state)


def kernel(x_prompt, x_sample, cache_k, cache_v, state_pool, state_gla, state_conv, norm1_g, w_in, w_gla_gate,
           b_gla_gate, gla_norm_g, w_pool, pool_scale, w_branch, w_out, norm2_g, w_up, conv_w, conv_b, w_down,
           final_g):
    bp, tp, d = x_prompt.shape
    bs, ts, _ = x_sample.shape
    depth = w_in.shape[0]
    past = cache_k.shape[2]
    bw, kw = d // 2, d // 4
    rank = w_gla_gate.shape[1]
    n_main = 4 * bw + 2 * kw + 2 * bw
    xp = x_prompt.reshape(bp * tp, d)
    xs = x_sample.reshape(bs * ts, d)
    outs_p, outs_s = [], []
    for l in range(depth):
        wl_pad = jnp.zeros((d, LANES), F32).at[:, :rank].set(w_in[l, :, n_main:n_main + rank])
        wg_pad = jnp.zeros((LANES, kw), F32).at[:rank, :].set(w_gla_gate[l])
        w = dict(
            g1=norm1_g[l].reshape(1, d), g2=norm2_g[l].reshape(1, d),
            wa=w_in[l, :, :n_main].astype(BF16), wl=wl_pad.astype(BF16), wg=wg_pad.astype(BF16),
            bg=b_gla_gate[l].reshape(1, kw),
            wgate=w_in[l, :, n_main + rank:].astype(BF16),
            w_pool=w_pool[l].astype(BF16), pool_scale=pool_scale[l].reshape(1, bw),
            gnorm=gla_norm_g[l].reshape(1, -1),
            wbr=w_branch[l].astype(BF16), wout=w_out[l].astype(BF16),
            wup=w_up[l].astype(BF16), cw=conv_w[l], cb=conv_b[l].reshape(1, -1), wdn=w_down[l].astype(BF16),
        )
        xp, st = _layer(xp, w, batch=bp, seq=tp, start=0, past_k=None, past_v=None,
                        prev_pool=None, prev_gla=None, prev_conv=None)
        outs_p.append(st)
        xs, st = _layer(xs, w, batch=bs, seq=ts, start=past,
                        past_k=cache_k[l].reshape(bs, past, bw), past_v=cache_v[l].reshape(bs, past, bw),
                        prev_pool=state_pool[l], prev_gla=state_gla[l], prev_conv=state_conv[l])
        outs_s.append(st)
    g = final_g.reshape(1, d)
    y_prompt = _final_norm(xp, g).reshape(bp, tp, d)
    y_sample = _final_norm(xs, g).reshape(bs, ts, d)
    stack = lambda outs, i: jnp.stack([o[i] for o in outs])
    return (y_prompt, y_sample,
            stack(outs_p, 0), stack(outs_p, 1), stack(outs_p, 2), stack(outs_p, 3), stack(outs_p, 4),
            stack(outs_s, 0), stack(outs_s, 1), stack(outs_s, 2), stack(outs_s, 3), stack(outs_s, 4))
```

```python
import functools

import jax
import jax.numpy as jnp
from jax import lax
from jax.experimental import pallas as pl
from jax.experimental.pallas import tpu as pltpu

F32 = jnp.float32
BF16 = jnp.bfloat16

NORM_EPS = 1e-6
LOG2E = 1.4426950408889634
CHUNK = 64
SB_HEAD_DIM = 64
POOL_WINDOWS = (2, 4, 8, 16)
POOL_STATE = max(POOL_WINDOWS) - 1
POOL_PAD = 16
GLA_HEADS = 4
GLA_GATE_TAU = 16.0
CONV_W = 3
CONV_PAD = 8
LANES = 128
V7X_SCOPED_VMEM_BYTES = 60000 * 1024

ROW_TILE = 512
SB_BLOCK = 256
SB_PAIRS = 4
SB_DEAD_LOG = -150.0


def _cparams(n_axes, vmem_bytes):
    return pltpu.CompilerParams(
        dimension_semantics=("arbitrary",) * n_axes,
        vmem_limit_bytes=int(min(vmem_bytes, V7X_SCOPED_VMEM_BYTES)),
    )


def _nbytes(shape, dtype):
    n = 1
    for s in shape:
        n *= s
    return n * jnp.dtype(dtype).itemsize


def _const_spec(shape):
    nd = len(shape)
    return pl.BlockSpec(shape, lambda *_: (0,) * nd, pipeline_mode=pl.Buffered(1))


def _rmsnorm(xf, g):
    ms = jnp.mean(xf * xf, axis=-1, keepdims=True)
    return (xf * lax.rsqrt(ms + NORM_EPS)) * g


def _log_sigmoid(x):
    return jnp.minimum(x, 0.0) - jnp.log1p(jnp.exp(-jnp.abs(x)))


def _split_bf16(x):
    hi = x.astype(BF16)
    lo = (x - hi.astype(F32)).astype(BF16)
    return hi, lo


def _dot(a, b):
    return jnp.dot(a, b, preferred_element_type=F32)


def _dot_nt(a, b):
    return lax.dot_general(a, b, (((1,), (1,)), ((), ())), preferred_element_type=F32)


def _dot_tn(a, b):
    return lax.dot_general(a, b, (((0,), (0,)), ((), ())), preferred_element_type=F32)


def _inproj_kernel(*refs, bw, kw, aliased):
    x_ref, g_ref, wa_ref, wl_ref, wg_ref, bg_ref = refs[:6]
    sb_ref, k_ref, v_ref, u_ref, gf_ref, vg_ref = refs[6 + 2 * aliased:]
    h = _rmsnorm(x_ref[...], g_ref[...]).astype(BF16)
    tm = x_ref.shape[0]
    heads = bw // SB_HEAD_DIM

    def proj(c0, c1):
        return _dot(h, wa_ref[:, c0:c1])

    def store_heads(dst_ref, val):
        for hd in range(heads):
            dst_ref[pl.ds(hd, tm, stride=heads), :] = val[:, hd * SB_HEAD_DIM:(hd + 1) * SB_HEAD_DIM]

    sb_ref[:, 0:bw] = (proj(0, bw) * (SB_HEAD_DIM ** -0.5 * LOG2E)).astype(BF16)
    k = proj(bw, 2 * bw)
    store_heads(k_ref, k)
    sb_ref[:, bw:2 * bw] = k.astype(BF16)
    v = proj(2 * bw, 3 * bw)
    store_heads(v_ref, v)
    sb_ref[:, 2 * bw:3 * bw] = v.astype(BF16)
    u_ref[...] = proj(3 * bw, 4 * bw)
    o = 4 * bw
    gf_ref[:, 0:kw] = proj(o, o + kw)
    gf_ref[:, kw:2 * kw] = proj(o + kw, o + 2 * kw)
    vg_ref[...] = proj(o + 2 * kw, o + 2 * kw + bw).astype(BF16)
    gf_ref[:, 3 * kw:3 * kw + bw] = proj(o + 2 * kw + bw, o + 2 * kw + 2 * bw)
    g_lr = _dot(h, wl_ref[...]).astype(BF16)
    pre = _dot(g_lr, wg_ref[...]) + bg_ref[...]
    gf_ref[:, 2 * kw:3 * kw] = _log_sigmoid(pre) * (1.0 / GLA_GATE_TAU)


def _inproj(x, g1, wa, wl, wg, bg, kstack, vstack, *, layer, depth, bw, kw):
    n, d = x.shape
    tm = min(ROW_TILE, n)
    gfw = 3 * kw + bw
    heads = bw // SB_HEAD_DIM
    aliased = kstack is not None
    kv_shape = jax.ShapeDtypeStruct((depth, n * heads, SB_HEAD_DIM), F32)
    out_shape = (
        jax.ShapeDtypeStruct((n, 3 * bw), BF16),
        kv_shape,
        kv_shape,
        jax.ShapeDtypeStruct((n, bw), F32),
        jax.ShapeDtypeStruct((n, gfw), F32),
        jax.ShapeDtypeStruct((n, bw), BF16),
    )
    row = lambda w: pl.BlockSpec((tm, w), lambda i: (i, 0))
    slab = pl.BlockSpec((None, tm * heads, SB_HEAD_DIM), lambda i: (layer, i, 0))
    blocks = 2 * (_nbytes((tm, d), F32) + _nbytes((tm, 3 * bw), BF16) + 2 * _nbytes((tm, heads, LANES), F32)
                  + _nbytes((tm, bw), F32) + _nbytes((tm, gfw), F32) + _nbytes((tm, bw), BF16))
    weights = _nbytes(wa.shape, BF16) + _nbytes(wl.shape, BF16) + _nbytes(wg.shape, BF16)
    temps = 4 * _nbytes((tm, d), F32)
    in_specs = [row(d), _const_spec(g1.shape), _const_spec(wa.shape), _const_spec(wl.shape),
                _const_spec(wg.shape), _const_spec(bg.shape)]
    args = [x, g1, wa, wl, wg, bg]
    if aliased:
        in_specs += [pl.BlockSpec(memory_space=pl.ANY)] * 2
        args += [kstack, vstack]
    return pl.pallas_call(
        functools.partial(_inproj_kernel, bw=bw, kw=kw, aliased=aliased),
        out_shape=out_shape,
        grid=(n // tm,),
        in_specs=in_specs,
        out_specs=(row(3 * bw), slab, slab, row(bw), row(gfw), row(bw)),
        input_output_aliases={6: 1, 7: 2} if aliased else {},
        compiler_params=_cparams(1, blocks + weights + temps),
        name="inproj",
    )(*args)


def _suffix_sum_matrix(n):
    r = lax.broadcasted_iota(jnp.int32, (n, n), 0)
    c = lax.broadcasted_iota(jnp.int32, (n, n), 1)
    return jnp.where(r > c, 1.0, 0.0).astype(BF16)


def _sb_chains(qs, kbs, vbs, u, carries, masks):
    n = len(qs)
    sign = jnp.uint32(0x80000000)
    ls_pos, ls_neg, ws, new, pvs = {}, {}, {}, [], []
    for t in range(n + 2):
        if t < n:
            s = _dot_nt(qs[t], kbs[t])
            nabs = pltpu.bitcast(pltpu.bitcast(s, jnp.uint32) | sign, F32)
            sp = jnp.log(1.0 + jnp.exp2(nabs)) * LOG2E
            pos = jnp.minimum(s, 0.0) - sp
            neg = pos - s
            ls_pos[t] = pos
            ls_neg[t] = neg if masks[t] is None else jnp.where(masks[t], neg, 0.0)
        i = t - 1
        if 0 <= i < n:
            later = _dot(ls_neg[i].astype(BF16), u)
            w = jnp.exp2(ls_pos.pop(i) + later + carries[i])
            ws[i] = (w if masks[i] is None else jnp.where(masks[i], w, 0.0)).astype(BF16)
            new.append(carries[i] + jnp.sum(ls_neg.pop(i), axis=-1, keepdims=True))
        i = t - 2
        if 0 <= i < n:
            pvs.append(_dot(ws.pop(i), vbs[i]))
    return pvs, new


def _head_masks():
    lane = lax.broadcasted_iota(jnp.int32, (1, LANES), 1)
    return lane < SB_HEAD_DIM


def _max_all(carries):
    m = carries[0]
    for c in carries[1:]:
        m = jnp.maximum(m, c)
    return jnp.max(m)


def _sb_prompt_kernel(q_ref, k_ref, v_ref, o_ref, acc_ref, carry_ref, *, blk):
    i = pl.program_id(2)
    npair = q_ref.shape[-1] // LANES
    heads = 2 * npair
    first = _head_masks()
    u = _suffix_sum_matrix(blk)
    r = lax.broadcasted_iota(jnp.int32, (blk, blk), 0)
    c = lax.broadcasted_iota(jnp.int32, (blk, blk), 1)
    causal = c < r
    qs = []
    for p in range(npair):
        q = q_ref[0, :, p * LANES:(p + 1) * LANES]
        qs += [jnp.where(first, q, jnp.zeros_like(q)), jnp.where(first, jnp.zeros_like(q), q)]

    def sweep(j, diagonal):
        off = pl.multiple_of(j * blk, blk)
        kbs = [k_ref[0, pl.ds(off, blk), p * LANES:(p + 1) * LANES] for p in range(npair)]
        vbs = [v_ref[0, pl.ds(off, blk), p * LANES:(p + 1) * LANES] for p in range(npair)]
        carries = [jnp.zeros((blk, 1), F32) if diagonal else carry_ref[hd] for hd in range(heads)]
        pvs, new = _sb_chains(qs, [kbs[hd // 2] for hd in range(heads)], [vbs[hd // 2] for hd in range(heads)],
                              u, carries, [causal if diagonal else None] * heads)
        for hd in range(heads):
            carry_ref[hd] = new[hd]
            if diagonal:
                acc_ref[hd] = pvs[hd]
            else:
                acc_ref[hd] += pvs[hd]
        return _max_all(new)

    top = sweep(i, True)

    def cond(state):
        return jnp.logical_and(state[0] >= 0, state[1] >= SB_DEAD_LOG)

    def body(state):
        return (state[0] - 1, sweep(state[0], False))

    lax.while_loop(cond, body, (i - 1, top))
    for p in range(npair):
        o_ref[0, :, p * LANES:(p + 1) * LANES] = jnp.where(first, acc_ref[2 * p], acc_ref[2 * p + 1]).astype(o_ref.dtype)


def _sb_prompt(sb, *, batch, seq, bw):
    blk = min(SB_BLOCK, seq)
    nq = seq // blk
    width = SB_PAIRS * LANES
    ngrp = bw // width
    sb3 = sb.reshape(batch, seq, 3 * bw)
    vmem = (2 * (2 * _nbytes((blk, width), BF16) + 2 * _nbytes((seq, width), BF16))
            + 4 * SB_PAIRS * _nbytes((blk, LANES), F32) + 2 * SB_PAIRS * 8 * _nbytes((blk, blk), F32))
    out = pl.pallas_call(
        functools.partial(_sb_prompt_kernel, blk=blk),
        out_shape=jax.ShapeDtypeStruct((batch, seq, bw), BF16),
        grid=(batch, ngrp, nq),
        in_specs=[
            pl.BlockSpec((1, blk, width), lambda b, p, i: (b, i, p)),
            pl.BlockSpec((1, seq, width), lambda b, p, i: (b, 0, ngrp + p)),
            pl.BlockSpec((1, seq, width), lambda b, p, i: (b, 0, 2 * ngrp + p)),
        ],
        out_specs=pl.BlockSpec((1, blk, width), lambda b, p, i: (b, i, p)),
        scratch_shapes=[pltpu.VMEM((2 * SB_PAIRS, blk, LANES), F32), pltpu.VMEM((2 * SB_PAIRS, blk, 1), F32)],
        compiler_params=_cparams(3, vmem),
        name="sb_prompt",
    )(sb3, sb3, sb3)
    return out.reshape(batch * seq, bw)


def _sb_sample_kernel(q_ref, kn_ref, vn_ref, kp_hbm, vp_hbm, o_ref, acc_ref, carry_ref, kbuf, vbuf, sem, *,
                      layer, tq, blk, npast):
    b = pl.program_id(0)
    heads = acc_ref.shape[0]
    npair = heads // 2
    first = _head_masks()
    u_new = _suffix_sum_matrix(tq)
    u = _suffix_sum_matrix(blk)
    r = lax.broadcasted_iota(jnp.int32, (tq, tq), 0)
    c = lax.broadcasted_iota(jnp.int32, (tq, tq), 1)
    causal = c < r

    def block_copies(j):
        rows = pl.ds(pl.multiple_of(j * (blk * heads), blk * heads), blk * heads)
        return (pltpu.make_async_copy(kp_hbm.at[layer, b, rows], kbuf, sem.at[0]),
                pltpu.make_async_copy(vp_hbm.at[layer, b, rows], vbuf, sem.at[1]))

    for cp in block_copies(npast - 1):
        cp.start()

    qs = []
    for p in range(npair):
        q = q_ref[0, :, p * LANES:(p + 1) * LANES]
        qs += [jnp.where(first, q, jnp.zeros_like(q)), jnp.where(first, jnp.zeros_like(q), q)]

    pair_lanes = [slice((hd // 2) * LANES, (hd // 2 + 1) * LANES) for hd in range(heads)]
    pvs, carries = _sb_chains(qs, [kn_ref[0, :, ln] for ln in pair_lanes], [vn_ref[0, :, ln] for ln in pair_lanes],
                              u_new, [jnp.zeros((tq, 1), F32)] * heads, [causal] * heads)
    for hd in range(heads):
        acc_ref[hd] = pvs[hd]
        carry_ref[hd] = carries[hd]

    def pair_operand(buf, p):
        lo = buf[pl.ds(2 * p, blk, stride=heads), :].astype(BF16)
        hi = buf[pl.ds(2 * p + 1, blk, stride=heads), :].astype(BF16)
        return jnp.concatenate([lo, hi], axis=1)

    def cond(state):
        return state[1] > 0

    def body(state):
        j = state[0]
        for cp in block_copies(j):
            cp.wait()
        kbs = [pair_operand(kbuf, p) for p in range(npair)]
        vbs = [pair_operand(vbuf, p) for p in range(npair)]
        pvs, new = _sb_chains(qs, [kbs[hd // 2] for hd in range(heads)], [vbs[hd // 2] for hd in range(heads)],
                              u, [carry_ref[hd] for hd in range(heads)], [None] * heads)
        for hd in range(heads):
            acc_ref[hd] += pvs[hd]
            carry_ref[hd] = new[hd]
        more = jnp.logical_and(j > 0, _max_all(new) >= SB_DEAD_LOG)

        @pl.when(more)
        def _():
            for cp in block_copies(j - 1):
                cp.start()

        return (j - 1, more.astype(jnp.int32))

    alive = _max_all(carries) >= SB_DEAD_LOG

    @pl.when(jnp.logical_not(alive))
    def _():
        for cp in block_copies(npast - 1):
            cp.wait()

    lax.while_loop(cond, body, (jnp.int32(npast - 1), alive.astype(jnp.int32)))
    for p in range(npair):
        o_ref[0, :, p * LANES:(p + 1) * LANES] = jnp.where(first, acc_ref[2 * p], acc_ref[2 * p + 1]).astype(o_ref.dtype)


def _sb_sample(sb, cache_k, cache_v, *, layer, batch, seq, bw):
    depth, _, past, heads, dh = cache_k.shape
    blk = min(SB_BLOCK, past)
    sb3 = sb.reshape(batch, seq, 3 * bw)
    flat = (depth, batch, past * heads, dh)
    vmem = (2 * 4 * _nbytes((seq, bw), BF16) + 2 * _nbytes((blk * heads, LANES), F32)
            + heads * _nbytes((seq, LANES), F32) + heads * 16 * _nbytes((seq, blk), F32))
    new = lambda col: pl.BlockSpec((1, seq, bw), lambda b: (b, 0, col))
    out = pl.pallas_call(
        functools.partial(_sb_sample_kernel, layer=layer, tq=seq, blk=blk, npast=past // blk),
        out_shape=jax.ShapeDtypeStruct((batch, seq, bw), BF16),
        grid=(batch,),
        in_specs=[new(0), new(1), new(2), pl.BlockSpec(memory_space=pl.ANY), pl.BlockSpec(memory_space=pl.ANY)],
        out_specs=new(0),
        scratch_shapes=[pltpu.VMEM((heads, seq, LANES), F32),
                        pltpu.VMEM((heads, seq, 1), F32),
                        pltpu.VMEM((blk * heads, dh), F32),
                        pltpu.VMEM((blk * heads, dh), F32),
                        pltpu.SemaphoreType.DMA((2,))],
        compiler_params=_cparams(1, vmem),
        name="sb_sample",
    )(sb3, sb3, sb3, cache_k.reshape(flat), cache_v.reshape(flat))
    return out.reshape(batch * seq, bw)


def _pool_kernel(*refs, tq, gdim, start, has_prev):
    if has_prev:
        u_ref, prev_ref, w_ref, sc_ref, o_ref, st_ref, ext_ref = refs
    else:
        u_ref, w_ref, sc_ref, o_ref, st_ref, ext_ref = refs
    t = pl.program_id(1)

    @pl.when(t == 0)
    def _():
        ext_ref[0:POOL_PAD, :] = jnp.zeros((POOL_PAD, ext_ref.shape[1]), F32)
        if has_prev:
            ext_ref[POOL_PAD - POOL_STATE:POOL_PAD, :] = prev_ref[0]

    @pl.when(t > 0)
    def _():
        ext_ref[0:POOL_PAD, :] = ext_ref[tq:tq + POOL_PAD, :]

    ext_ref[POOL_PAD:POOL_PAD + tq, :] = u_ref[0]
    pos = start + t * tq + lax.broadcasted_iota(jnp.int32, (tq, 1), 0)
    for g, w in enumerate(POOL_WINDOWS):
        cols = slice(g * gdim, (g + 1) * gdim)
        tok = ext_ref[POOL_PAD:POOL_PAD + tq, cols]
        s = tok
        for d in range(1, w):
            s = s + ext_ref[POOL_PAD - d:POOL_PAD - d + tq, cols]
        cnt = jnp.minimum(pos + 1, w).astype(F32)
        pooled = (s / cnt - tok).astype(BF16)
        o_ref[0, :, cols] = (_dot(pooled, w_ref[g]) * sc_ref[:, cols]).astype(o_ref.dtype)

    @pl.when(t == pl.num_programs(1) - 1)
    def _():
        st_ref[0] = ext_ref[POOL_PAD + tq - POOL_STATE:POOL_PAD + tq, :]


def _pool(u, prev, w_pool, scale, *, batch, seq, start):
    bw = u.shape[-1]
    tq = min(ROW_TILE, seq)
    gdim = bw // len(POOL_WINDOWS)
    has_prev = prev is not None
    u3 = u.reshape(batch, seq, bw)
    in_specs = [pl.BlockSpec((1, tq, bw), lambda b, t: (b, t, 0))]
    args = [u3]
    if has_prev:
        in_specs.append(pl.BlockSpec((1, POOL_STATE, bw), lambda b, t: (b, 0, 0)))
        args.append(prev)
    in_specs += [_const_spec(w_pool.shape), _const_spec(scale.shape)]
    args += [w_pool, scale]
    vmem = (2 * (_nbytes((tq, bw), F32) + _nbytes((tq, bw), BF16) + 2 * _nbytes((POOL_PAD, bw), F32))
            + _nbytes((POOL_PAD + tq, bw), F32) + _nbytes(w_pool.shape, BF16) + 6 * _nbytes((tq, bw), F32))
    o, st = pl.pallas_call(
        functools.partial(_pool_kernel, tq=tq, gdim=gdim, start=start, has_prev=has_prev),
        out_shape=(jax.ShapeDtypeStruct((batch, seq, bw), BF16),
                   jax.ShapeDtypeStruct((batch, POOL_STATE, bw), F32)),
        grid=(batch, seq // tq),
        in_specs=in_specs,
        out_specs=(pl.BlockSpec((1, tq, bw), lambda b, t: (b, t, 0)),
                   pl.BlockSpec((1, POOL_STATE, bw), lambda b, t: (b, 0, 0))),
        scratch_shapes=[pltpu.VMEM((POOL_PAD + tq, bw), F32)],
        compiler_params=_cparams(2, vmem),
        name="pool",
    )(*args)
    return o.reshape(batch * seq, bw), st


def _gla_kernel(*refs, tt, kw, vw, has_state):
    if has_state:
        gf_ref, vg_ref, s0_ref, gn_ref, o_ref, sT_out_ref, sT_ref, og_ref = refs
    else:
        gf_ref, vg_ref, gn_ref, o_ref, sT_out_ref, sT_ref, og_ref = refs
    t = pl.program_id(1)
    dk = kw // GLA_HEADS
    dv = vw // GLA_HEADS

    @pl.when(t == 0)
    def _():
        if has_state:
            sT_ref[...] = s0_ref[0]
        else:
            sT_ref[...] = jnp.zeros(sT_ref.shape, F32)

    def block_mask(shape, rdiv, cdiv):
        r = lax.broadcasted_iota(jnp.int32, shape, 0) // rdiv
        c = lax.broadcasted_iota(jnp.int32, shape, 1) // cdiv
        return r == c

    r = lax.broadcasted_iota(jnp.int32, (CHUNK, CHUNK), 0)
    c = lax.broadcasted_iota(jnp.int32, (CHUNK, CHUNK), 1)
    tril = jnp.where(r >= c, 1.0, 0.0).astype(BF16)
    m_kk = block_mask((GLA_HEADS * CHUNK, kw), CHUNK, dk)
    m_vv = block_mask((GLA_HEADS * CHUNK, vw), CHUNK, dv)
    m_st = block_mask((vw, kw), dv, dk)
    qi = lax.broadcasted_iota(jnp.int32, (CHUNK, GLA_HEADS * CHUNK), 0)
    kj = lax.broadcasted_iota(jnp.int32, (CHUNK, GLA_HEADS * CHUNK), 1) % CHUNK
    causal = qi >= kj

    for ci in range(tt // CHUNK):
        rows = slice(ci * CHUNK, (ci + 1) * CHUNK)
        q = gf_ref[0, rows, 0:kw] * (dk ** -0.5)
        k = gf_ref[0, rows, kw:2 * kw]
        la = gf_ref[0, rows, 2 * kw:3 * kw]
        v = vg_ref[0, rows, :]
        hi, lo = _split_bf16(la)
        b = _dot(tril, hi) + _dot(tril, lo)
        bref = b[CHUNK // 2:CHUNK // 2 + 1, :]
        blast = b[CHUNK - 1:CHUNK, :]
        qe = (q * jnp.exp(b - bref)).astype(BF16)
        ke = (k * jnp.exp(bref - b)).astype(BF16)
        qb = (q * jnp.exp(b)).astype(BF16)
        kl = (k * jnp.exp(blast - b)).astype(BF16)
        ke_bd = jnp.where(m_kk, jnp.concatenate([ke] * GLA_HEADS, axis=0), jnp.zeros((), BF16))
        att = jnp.where(causal, _dot_nt(qe, ke_bd), 0.0).astype(BF16)
        v_bd = jnp.where(m_vv, jnp.concatenate([v] * GLA_HEADS, axis=0), jnp.zeros((), BF16))
        s_t = sT_ref[...]
        og_ref[rows, :] = _dot(att, v_bd) + _dot_nt(qb, s_t.astype(BF16))
        upd = jnp.where(m_st, _dot_tn(v, kl), 0.0)
        sT_ref[...] = s_t * jnp.exp(blast) + upd

    gn = gn_ref[...]
    for h in range(GLA_HEADS):
        cols = slice(h * dv, (h + 1) * dv)
        y = _rmsnorm(og_ref[:, cols], gn)
        rg = gf_ref[0, :, 3 * kw + h * dv:3 * kw + (h + 1) * dv]
        o_ref[0, :, cols] = (y * (rg / (1.0 + jnp.exp(-rg)))).astype(o_ref.dtype)

    @pl.when(t == pl.num_programs(1) - 1)
    def _():
        sT_out_ref[0] = sT_ref[...]


def _gla(gf, vg, s0, gnorm, *, batch, seq, kw, vw):
    tt = min(ROW_TILE, seq)
    dk, dv = kw // GLA_HEADS, vw // GLA_HEADS
    gfw = gf.shape[-1]
    has_state = s0 is not None
    in_specs = [pl.BlockSpec((1, tt, gfw), lambda b, t: (b, t, 0)),
                pl.BlockSpec((1, tt, vw), lambda b, t: (b, t, 0))]
    args = [gf.reshape(batch, seq, gfw), vg.reshape(batch, seq, vw)]
    if has_state:
        s0t = jnp.zeros((batch, GLA_HEADS, dv, GLA_HEADS, dk), F32)
        for h in range(GLA_HEADS):
            s0t = s0t.at[:, h, :, h, :].set(jnp.swapaxes(s0[:, h], 1, 2))
        s0t = s0t.reshape(batch, vw, kw)
        in_specs.append(pl.BlockSpec((1, vw, kw), lambda b, t: (b, 0, 0)))
        args.append(s0t)
    in_specs.append(_const_spec(gnorm.shape))
    args.append(gnorm)
    vmem = (2 * (_nbytes((tt, gfw), F32) + 2 * _nbytes((tt, vw), BF16) + 2 * _nbytes((vw, kw), F32))
            + _nbytes((vw, kw), F32) + _nbytes((tt, vw), F32) + 24 * _nbytes((GLA_HEADS * CHUNK, vw), F32))
    o, st = pl.pallas_call(
        functools.partial(_gla_kernel, tt=tt, kw=kw, vw=vw, has_state=has_state),
        out_shape=(jax.ShapeDtypeStruct((batch, seq, vw), BF16),
                   jax.ShapeDtypeStruct((batch, vw, kw), F32)),
        grid=(batch, seq // tt),
        in_specs=in_specs,
        out_specs=(pl.BlockSpec((1, tt, vw), lambda b, t: (b, t, 0)),
                   pl.BlockSpec((1, vw, kw), lambda b, t: (b, 0, 0))),
        scratch_shapes=[pltpu.VMEM((vw, kw), F32), pltpu.VMEM((tt, vw), F32)],
        compiler_params=_cparams(2, vmem),
        name="gla",
    )(*args)
    st5 = st.reshape(batch, GLA_HEADS, dv, GLA_HEADS, dk)
    state = jnp.stack([st5[:, h, :, h, :] for h in range(GLA_HEADS)], axis=1)
    return o.reshape(batch * seq, vw), jnp.swapaxes(state, 2, 3)


def _merge_kernel(x_ref, g_ref, osb_ref, opool_ref, ogla_ref, wgate_ref, wbr_ref, wout_ref, o_ref, *, d):
    xf = x_ref[...]
    h = _rmsnorm(xf, g_ref[...]).astype(BF16)
    m = None
    for n, br_ref in enumerate((osb_ref, opool_ref, ogla_ref)):
        logits = _dot(h, wgate_ref[:, n * d:(n + 1) * d])
        gate = 1.0 / (1.0 + jnp.exp(-logits))
        term = gate * _dot(br_ref[...], wbr_ref[n])
        m = term if m is None else m + term
    o_ref[...] = xf + _dot(m.astype(BF16), wout_ref[...])


def _merge(x, g1, o_sb, o_pool, o_gla, wgate, wbr, wout):
    n, d = x.shape
    bw = o_sb.shape[-1]
    tm = min(ROW_TILE, n)
    row = lambda w: pl.BlockSpec((tm, w), lambda i: (i, 0))
    vmem = (2 * (2 * _nbytes((tm, d), F32) + 3 * _nbytes((tm, bw), BF16))
            + _nbytes(wgate.shape, BF16) + _nbytes(wbr.shape, BF16) + _nbytes(wout.shape, BF16)
            + 6 * _nbytes((tm, d), F32))
    return pl.pallas_call(
        functools.partial(_merge_kernel, d=d),
        out_shape=jax.ShapeDtypeStruct((n, d), F32),
        grid=(n // tm,),
        in_specs=[row(d), _const_spec(g1.shape), row(bw), row(bw), row(bw),
                  _const_spec(wgate.shape), _const_spec(wbr.shape), _const_spec(wout.shape)],
        out_specs=row(d),
        compiler_params=_cparams(1, vmem),
        name="merge",
    )(x, g1, o_sb, o_pool, o_gla, wgate, wbr, wout)


FFN_COLS = 256


def _gelu_tanh(x):
    c = 0.7978845608028654
    return 0.5 * x * (1.0 + jnp.tanh(c * (x + 0.044715 * (x * x * x))))


def _ffn_kernel(*refs, nseg, seg, dff, has_prev):
    if has_prev:
        x_ref, g_ref, prev_ref, wup_ref, cw_ref, cb_ref, wdn_ref, o_ref, st_ref, ext_ref, gact_ref, tail_ref = refs
    else:
        x_ref, g_ref, wup_ref, cw_ref, cb_ref, wdn_ref, o_ref, st_ref, ext_ref, gact_ref, tail_ref = refs
    t = pl.program_id(1)
    xf = x_ref[...]
    h = _rmsnorm(xf, g_ref[...]).astype(BF16)
    pad0 = CONV_PAD - (CONV_W - 1)

    if not has_prev:
        @pl.when(t == 0)
        def _():
            tail_ref[...] = jnp.zeros(tail_ref.shape, F32)

    def conv_half(c0):
        cols = slice(c0, c0 + FFN_COLS)
        up = _dot(h, wup_ref[:, cols])
        outs = []
        for s in range(nseg):
            if has_prev:
                ext_ref[s, pad0:CONV_PAD, :] = prev_ref[s, :, cols]
            else:
                ext_ref[s, 0:CONV_PAD, :] = tail_ref[:, cols]
            ext_ref[s, CONV_PAD:CONV_PAD + seg, :] = up[s * seg:(s + 1) * seg, :]
            acc = cb_ref[:, cols] + ext_ref[s, pad0:pad0 + seg, :] * cw_ref[0:1, cols]
            for i in range(1, CONV_W):
                acc = acc + ext_ref[s, pad0 + i:pad0 + i + seg, :] * cw_ref[i:i + 1, cols]
            outs.append(acc)
            if has_prev:
                st_ref[s, :, cols] = ext_ref[s, CONV_PAD + seg - (CONV_W - 1):CONV_PAD + seg, :]
            else:
                tail_ref[:, cols] = ext_ref[s, seg:seg + CONV_PAD, :]
        return outs[0] if nseg == 1 else jnp.concatenate(outs, axis=0)

    for c0 in range(0, dff, FFN_COLS):
        a = conv_half(c0)
        b = conv_half(dff + c0)
        gact_ref[:, c0:c0 + FFN_COLS] = (_gelu_tanh(a) * b).astype(BF16)

    o_ref[...] = xf + _dot(gact_ref[...], wdn_ref[...])

    if not has_prev:
        @pl.when(t == pl.num_programs(1) - 1)
        def _():
            st_ref[0] = tail_ref[CONV_PAD - (CONV_W - 1):CONV_PAD, :]


def _ffn(x, g2, prev, wup, cw, cb, wdn, *, batch, seq):
    n, d = x.shape
    dff = wdn.shape[0]
    has_prev = prev is not None
    if has_prev:
        nseg = max(1, min(ROW_TILE // seq, batch))
        seg = seq
        grid = (batch // nseg, 1)
        xmap = lambda i, t: (i, 0)
        stmap = lambda i, t: (i, 0, 0)
    else:
        nseg = 1
        seg = min(ROW_TILE, seq)
        grid = (batch, seq // seg)
        nt = seq // seg
        xmap = lambda b, t: (b * nt + t, 0)
        stmap = lambda b, t: (b, 0, 0)
    tm = nseg * seg
    in_specs = [pl.BlockSpec((tm, d), xmap), _const_spec(g2.shape)]
    args = [x, g2]
    if has_prev:
        in_specs.append(pl.BlockSpec((nseg, CONV_W - 1, 2 * dff), stmap))
        args.append(prev)
    in_specs += [_const_spec(wup.shape), _const_spec(cw.shape), _const_spec(cb.shape), _const_spec(wdn.shape)]
    args += [wup, cw, cb, wdn]
    vmem = (2 * (2 * _nbytes((tm, d), F32) + 2 * _nbytes((nseg, CONV_W - 1, 2 * dff), F32))
            + _nbytes(wup.shape, BF16) + _nbytes(wdn.shape, BF16) + 2 * _nbytes(cw.shape, F32)
            + _nbytes((nseg, CONV_PAD + seg, FFN_COLS), F32) + _nbytes((tm, dff), BF16)
            + _nbytes((CONV_PAD, 2 * dff), F32) + 4 * _nbytes((tm, d), F32) + 12 * _nbytes((tm, FFN_COLS), F32))
    o, st = pl.pallas_call(
        functools.partial(_ffn_kernel, nseg=nseg, seg=seg, dff=dff, has_prev=has_prev),
        out_shape=(jax.ShapeDtypeStruct((n, d), F32),
                   jax.ShapeDtypeStruct((batch, CONV_W - 1, 2 * dff), F32)),
        grid=grid,
        in_specs=in_specs,
        out_specs=(pl.BlockSpec((tm, d), xmap), pl.BlockSpec((nseg, CONV_W - 1, 2 * dff), stmap)),
        scratch_shapes=[pltpu.VMEM((nseg, CONV_PAD + seg, FFN_COLS), F32),
                        pltpu.VMEM((tm, dff), BF16),
                        pltpu.VMEM((CONV_PAD, 2 * dff), F32)],
        compiler_params=_cparams(2, vmem),
        name="ffn",
    )(*args)
    return o, st


def _final_norm_kernel(x_ref, g_ref, o_ref):
    o_ref[...] = _rmsnorm(x_ref[...], g_ref[...])


def _final_norm(x, g):
    n, d = x.shape
    tm = min(ROW_TILE, n)
    return pl.pallas_call(
        _final_norm_kernel,
        out_shape=jax.ShapeDtypeStruct((n, d), F32),
        grid=(n // tm,),
        in_specs=[pl.BlockSpec((tm, d), lambda i: (i, 0)), _const_spec(g.shape)],
        out_specs=pl.BlockSpec((tm, d), lambda i: (i, 0)),
        compiler_params=_cparams(1, 6 * _nbytes((tm, d), F32)),
        name="final_norm",
    )(x, g)


def _layer(x, w, kstack, vstack, *, layer, depth, batch, seq, start, cache_k, cache_v, prev_pool, prev_gla,
           prev_conv):
    d = x.shape[-1]
    bw, kw = d // 2, d // 4
    sb, kstack, vstack, u, gf, vg = _inproj(x, w["g1"], w["wa"], w["wl"], w["wg"], w["bg"], kstack, vstack,
                                            layer=layer, depth=depth, bw=bw, kw=kw)
    if cache_k is None:
        o_sb = _sb_prompt(sb, batch=batch, seq=seq, bw=bw)
    else:
        o_sb = _sb_sample(sb, cache_k, cache_v, layer=layer, batch=batch, seq=seq, bw=bw)
    o_pool, pool_state = _pool(u, prev_pool, w["w_pool"], w["pool_scale"], batch=batch, seq=seq, start=start)
    o_gla, gla_state = _gla(gf, vg, prev_gla, w["gnorm"], batch=batch, seq=seq, kw=kw, vw=bw)
    x = _merge(x, w["g1"], o_sb, o_pool, o_gla, w["wgate"], w["wbr"], w["wout"])
    x, conv_state = _ffn(x, w["g2"], prev_conv, w["wup"], w["cw"], w["cb"], w["wdn"], batch=batch, seq=seq)
    return x, kstack, vstack, (pool_state, gla_state, conv_state)


def kernel(x_prompt, x_sample, cache_k, cache_v, state_pool, state_gla, state_conv, norm1_g, w_in, w_gla_gate,
           b_gla_gate, gla_norm_g, w_pool, pool_scale, w_branch, w_out, norm2_g, w_up, conv_w, conv_b, w_down,
           final_g):
    bp, tp, d = x_prompt.shape
    bs, ts, _ = x_sample.shape
    depth = w_in.shape[0]
    past = cache_k.shape[2]
    bw, kw = d // 2, d // 4
    rank = w_gla_gate.shape[1]
    n_main = 4 * bw + 2 * kw + 2 * bw
    xp = x_prompt.reshape(bp * tp, d)
    xs = x_sample.reshape(bs * ts, d)
    outs_p, outs_s = [], []
    kp = vp = ks = vs = None
    for l in range(depth):
        wl_pad = jnp.zeros((d, LANES), F32).at[:, :rank].set(w_in[l, :, n_main:n_main + rank])
        wg_pad = jnp.zeros((LANES, kw), F32).at[:rank, :].set(w_gla_gate[l])
        w = dict(
            g1=norm1_g[l].reshape(1, d), g2=norm2_g[l].reshape(1, d),
            wa=w_in[l, :, :n_main].astype(BF16), wl=wl_pad.astype(BF16), wg=wg_pad.astype(BF16),
            bg=b_gla_gate[l].reshape(1, kw),
            wgate=w_in[l, :, n_main + rank:].astype(BF16),
            w_pool=w_pool[l].astype(BF16), pool_scale=pool_scale[l].reshape(1, bw),
            gnorm=gla_norm_g[l].reshape(1, -1),
            wbr=w_branch[l].astype(BF16), wout=w_out[l].astype(BF16),
            wup=w_up[l].astype(BF16), cw=conv_w[l], cb=conv_b[l].reshape(1, -1), wdn=w_down[l].astype(BF16),
        )
        xp, kp, vp, st = _layer(xp, w, kp, vp, layer=l, depth=depth, batch=bp, seq=tp, start=0,
                                cache_k=None, cache_v=None, prev_pool=None, prev_gla=None, prev_conv=None)
        outs_p.append(st)
        xs, ks, vs, st = _layer(xs, w, ks, vs, layer=l, depth=depth, batch=bs, seq=ts, start=past,
                                cache_k=cache_k, cache_v=cache_v, prev_pool=state_pool[l], prev_gla=state_gla[l],
                                prev_conv=state_conv[l])
        outs_s.append(st)
    g = final_g.reshape(1, d)
    y_prompt = _final_norm(xp, g).reshape(bp, tp, d)
    y_sample = _final_norm(xs, g).reshape(bs, ts, d)
    stack = lambda outs, i: jnp.stack([o[i] for o in outs])
    heads = bw // SB_HEAD_DIM
    kv_p = (depth, bp, tp, heads, SB_HEAD_DIM)
    kv_s = (depth, bs, ts, heads, SB_HEAD_DIM)
    return (y_prompt, y_sample,
            kp.reshape(kv_p), vp.reshape(kv_p), stack(outs_p, 0), stack(outs_p, 1), stack(outs_p, 2),
            ks.reshape(kv_s), vs.reshape(kv_s), stack(outs_s, 0), stack(outs_s, 1), stack(outs_s, 2))
```

```python
import functools

import jax
import jax.numpy as jnp
from jax import lax
from jax.experimental import pallas as pl
from jax.experimental.pallas import tpu as pltpu

F32 = jnp.float32
BF16 = jnp.bfloat16

NORM_EPS = 1e-6
LOG2E = 1.4426950408889634
CHUNK = 64
SB_HEAD_DIM = 64
POOL_WINDOWS = (2, 4, 8, 16)
POOL_STATE = max(POOL_WINDOWS) - 1
POOL_PAD = 16
GLA_HEADS = 4
GLA_GATE_TAU = 16.0
CONV_W = 3
CONV_PAD = 8
LANES = 128
V7X_SCOPED_VMEM_BYTES = 60000 * 1024

ROW_TILE = 512
SB_BLOCK = 256
SB_PAIRS = 4
SB_DEAD_LOG = -150.0


def _cparams(n_axes, vmem_bytes):
    return pltpu.CompilerParams(
        dimension_semantics=("arbitrary",) * n_axes,
        vmem_limit_bytes=int(min(vmem_bytes, V7X_SCOPED_VMEM_BYTES)),
    )


def _nbytes(shape, dtype):
    n = 1
    for s in shape:
        n *= s
    return n * jnp.dtype(dtype).itemsize


def _const_spec(shape):
    nd = len(shape)
    return pl.BlockSpec(shape, lambda *_: (0,) * nd, pipeline_mode=pl.Buffered(1))


def _rmsnorm(xf, g):
    ms = jnp.mean(xf * xf, axis=-1, keepdims=True)
    return (xf * lax.rsqrt(ms + NORM_EPS)) * g


def _log_sigmoid(x):
    return jnp.minimum(x, 0.0) - jnp.log1p(jnp.exp(-jnp.abs(x)))


def _split_bf16(x):
    hi = x.astype(BF16)
    lo = (x - hi.astype(F32)).astype(BF16)
    return hi, lo


def _dot(a, b):
    return jnp.dot(a, b, preferred_element_type=F32)


def _dot_nt(a, b):
    return lax.dot_general(a, b, (((1,), (1,)), ((), ())), preferred_element_type=F32)


def _dot_tn(a, b):
    return lax.dot_general(a, b, (((0,), (0,)), ((), ())), preferred_element_type=F32)


def _inproj_kernel(*refs, bw, kw, aliased):
    x_ref, g_ref, wa_ref, wl_ref, wg_ref, bg_ref = refs[:6]
    sb_ref, k_ref, v_ref, u_ref, gf_ref, vg_ref = refs[6 + 2 * aliased:]
    h = _rmsnorm(x_ref[...], g_ref[...]).astype(BF16)
    tm = x_ref.shape[0]
    heads = bw // SB_HEAD_DIM

    def proj(c0, c1):
        return _dot(h, wa_ref[:, c0:c1])

    def store_heads(dst_ref, val):
        for hd in range(heads):
            dst_ref[pl.ds(hd, tm, stride=heads), :] = val[:, hd * SB_HEAD_DIM:(hd + 1) * SB_HEAD_DIM]

    sb_ref[:, 0:bw] = (proj(0, bw) * (SB_HEAD_DIM ** -0.5 * LOG2E)).astype(BF16)
    k = proj(bw, 2 * bw)
    store_heads(k_ref, k)
    sb_ref[:, bw:2 * bw] = k.astype(BF16)
    v = proj(2 * bw, 3 * bw)
    store_heads(v_ref, v)
    sb_ref[:, 2 * bw:3 * bw] = v.astype(BF16)
    u_ref[...] = proj(3 * bw, 4 * bw)
    o = 4 * bw
    gf_ref[:, 0:kw] = proj(o, o + kw)
    gf_ref[:, kw:2 * kw] = proj(o + kw, o + 2 * kw)
    vg_ref[...] = proj(o + 2 * kw, o + 2 * kw + bw).astype(BF16)
    gf_ref[:, 3 * kw:3 * kw + bw] = proj(o + 2 * kw + bw, o + 2 * kw + 2 * bw)
    g_lr = _dot(h, wl_ref[...]).astype(BF16)
    pre = _dot(g_lr, wg_ref[...]) + bg_ref[...]
    gf_ref[:, 2 * kw:3 * kw] = _log_sigmoid(pre) * (1.0 / GLA_GATE_TAU)


def _inproj(x, g1, wa, wl, wg, bg, kstack, vstack, *, layer, depth, bw, kw):
    n, d = x.shape
    tm = min(ROW_TILE, n)
    gfw = 3 * kw + bw
    heads = bw // SB_HEAD_DIM
    aliased = kstack is not None
    kv_shape = jax.ShapeDtypeStruct((depth, n * heads, SB_HEAD_DIM), F32)
    out_shape = (
        jax.ShapeDtypeStruct((n, 3 * bw), BF16),
        kv_shape,
        kv_shape,
        jax.ShapeDtypeStruct((n, bw), F32),
        jax.ShapeDtypeStruct((n, gfw), F32),
        jax.ShapeDtypeStruct((n, bw), BF16),
    )
    row = lambda w: pl.BlockSpec((tm, w), lambda i: (i, 0))
    slab = pl.BlockSpec((None, tm * heads, SB_HEAD_DIM), lambda i: (layer, i, 0))
    blocks = 2 * (_nbytes((tm, d), F32) + _nbytes((tm, 3 * bw), BF16) + 2 * _nbytes((tm, heads, LANES), F32)
                  + _nbytes((tm, bw), F32) + _nbytes((tm, gfw), F32) + _nbytes((tm, bw), BF16))
    weights = _nbytes(wa.shape, BF16) + _nbytes(wl.shape, BF16) + _nbytes(wg.shape, BF16)
    temps = 4 * _nbytes((tm, d), F32)
    in_specs = [row(d), _const_spec(g1.shape), _const_spec(wa.shape), _const_spec(wl.shape),
                _const_spec(wg.shape), _const_spec(bg.shape)]
    args = [x, g1, wa, wl, wg, bg]
    if aliased:
        in_specs += [pl.BlockSpec(memory_space=pl.ANY)] * 2
        args += [kstack, vstack]
    return pl.pallas_call(
        functools.partial(_inproj_kernel, bw=bw, kw=kw, aliased=aliased),
        out_shape=out_shape,
        grid=(n // tm,),
        in_specs=in_specs,
        out_specs=(row(3 * bw), slab, slab, row(bw), row(gfw), row(bw)),
        input_output_aliases={6: 1, 7: 2} if aliased else {},
        compiler_params=_cparams(1, blocks + weights + temps),
        name="inproj",
    )(*args)


def _suffix_sum_matrix(n):
    r = lax.broadcasted_iota(jnp.int32, (n, n), 0)
    c = lax.broadcasted_iota(jnp.int32, (n, n), 1)
    return jnp.where(r > c, 1.0, 0.0).astype(BF16)


def _sb_chains(qs, kbs, vbs, u, carries, masks, keys_on_lanes=False):
    n = len(qs)
    sign = jnp.uint32(0x80000000)
    ls_pos, ls_neg, ws, new, pvs = {}, {}, {}, [], []
    for t in range(n + 2):
        if t < n:
            s = _dot(qs[t], kbs[t]) if keys_on_lanes else _dot_nt(qs[t], kbs[t])
            nabs = pltpu.bitcast(pltpu.bitcast(s, jnp.uint32) | sign, F32)
            sp = jnp.log(1.0 + jnp.exp2(nabs)) * LOG2E
            pos = jnp.minimum(s, 0.0) - sp
            neg = pos - s
            ls_pos[t] = pos
            ls_neg[t] = neg if masks[t] is None else jnp.where(masks[t], neg, 0.0)
        i = t - 1
        if 0 <= i < n:
            later = _dot(ls_neg[i].astype(BF16), u)
            w = jnp.exp2(ls_pos.pop(i) + later + carries[i])
            ws[i] = (w if masks[i] is None else jnp.where(masks[i], w, 0.0)).astype(BF16)
            new.append(carries[i] + jnp.sum(ls_neg.pop(i), axis=-1, keepdims=True))
        i = t - 2
        if 0 <= i < n:
            pvs.append(_dot_nt(ws.pop(i), vbs[i]) if keys_on_lanes else _dot(ws.pop(i), vbs[i]))
    return pvs, new


def _head_masks():
    lane = lax.broadcasted_iota(jnp.int32, (1, LANES), 1)
    return lane < SB_HEAD_DIM


def _max_all(carries):
    m = carries[0]
    for c in carries[1:]:
        m = jnp.maximum(m, c)
    return jnp.max(m)


def _sb_prompt_kernel(q_ref, k_ref, v_ref, o_ref, acc_ref, carry_ref, *, blk):
    i = pl.program_id(2)
    npair = q_ref.shape[-1] // LANES
    heads = 2 * npair
    first = _head_masks()
    u = _suffix_sum_matrix(blk)
    r = lax.broadcasted_iota(jnp.int32, (blk, blk), 0)
    c = lax.broadcasted_iota(jnp.int32, (blk, blk), 1)
    causal = c < r
    qs = []
    for p in range(npair):
        q = q_ref[0, :, p * LANES:(p + 1) * LANES]
        qs += [jnp.where(first, q, jnp.zeros_like(q)), jnp.where(first, jnp.zeros_like(q), q)]

    def sweep(j, diagonal):
        off = pl.multiple_of(j * blk, blk)
        kbs = [k_ref[0, pl.ds(off, blk), p * LANES:(p + 1) * LANES] for p in range(npair)]
        vbs = [v_ref[0, pl.ds(off, blk), p * LANES:(p + 1) * LANES] for p in range(npair)]
        carries = [jnp.zeros((blk, 1), F32) if diagonal else carry_ref[hd] for hd in range(heads)]
        pvs, new = _sb_chains(qs, [kbs[hd // 2] for hd in range(heads)], [vbs[hd // 2] for hd in range(heads)],
                              u, carries, [causal if diagonal else None] * heads)
        for hd in range(heads):
            carry_ref[hd] = new[hd]
            if diagonal:
                acc_ref[hd] = pvs[hd]
            else:
                acc_ref[hd] += pvs[hd]
        return _max_all(new)

    top = sweep(i, True)

    def cond(state):
        return jnp.logical_and(state[0] >= 0, state[1] >= SB_DEAD_LOG)

    def body(state):
        return (state[0] - 1, sweep(state[0], False))

    lax.while_loop(cond, body, (i - 1, top))
    for p in range(npair):
        o_ref[0, :, p * LANES:(p + 1) * LANES] = jnp.where(first, acc_ref[2 * p], acc_ref[2 * p + 1]).astype(o_ref.dtype)


def _sb_prompt(sb, *, batch, seq, bw):
    blk = min(SB_BLOCK, seq)
    nq = seq // blk
    width = SB_PAIRS * LANES
    ngrp = bw // width
    sb3 = sb.reshape(batch, seq, 3 * bw)
    vmem = (2 * (2 * _nbytes((blk, width), BF16) + 2 * _nbytes((seq, width), BF16))
            + 4 * SB_PAIRS * _nbytes((blk, LANES), F32) + 2 * SB_PAIRS * 8 * _nbytes((blk, blk), F32))
    out = pl.pallas_call(
        functools.partial(_sb_prompt_kernel, blk=blk),
        out_shape=jax.ShapeDtypeStruct((batch, seq, bw), BF16),
        grid=(batch, ngrp, nq),
        in_specs=[
            pl.BlockSpec((1, blk, width), lambda b, p, i: (b, i, p)),
            pl.BlockSpec((1, seq, width), lambda b, p, i: (b, 0, ngrp + p)),
            pl.BlockSpec((1, seq, width), lambda b, p, i: (b, 0, 2 * ngrp + p)),
        ],
        out_specs=pl.BlockSpec((1, blk, width), lambda b, p, i: (b, i, p)),
        scratch_shapes=[pltpu.VMEM((2 * SB_PAIRS, blk, LANES), F32), pltpu.VMEM((2 * SB_PAIRS, blk, 1), F32)],
        compiler_params=_cparams(3, vmem),
        name="sb_prompt",
    )(sb3, sb3, sb3)
    return out.reshape(batch * seq, bw)


def _sb_sample_kernel(q_ref, kn_ref, vn_ref, kp_hbm, vp_hbm, o_ref, acc_ref, carry_ref, kbuf, vbuf, sem, *,
                      layer, tq, blk, npast):
    b = pl.program_id(0)
    heads = acc_ref.shape[0]
    npair = heads // 2
    first = _head_masks()
    u_new = _suffix_sum_matrix(tq)
    u = _suffix_sum_matrix(blk)
    r = lax.broadcasted_iota(jnp.int32, (tq, tq), 0)
    c = lax.broadcasted_iota(jnp.int32, (tq, tq), 1)
    causal = c < r

    def block_copies(j):
        rows = pl.ds(pl.multiple_of(j * blk, blk), blk)
        return (pltpu.make_async_copy(kp_hbm.at[layer, b, :, rows], kbuf, sem.at[0]),
                pltpu.make_async_copy(vp_hbm.at[layer, b, :, rows], vbuf, sem.at[1]))

    for cp in block_copies(npast - 1):
        cp.start()

    qs = []
    for p in range(npair):
        q = q_ref[0, :, p * LANES:(p + 1) * LANES]
        qs += [jnp.where(first, q, jnp.zeros_like(q)), jnp.where(first, jnp.zeros_like(q), q)]

    pair_lanes = [slice((hd // 2) * LANES, (hd // 2 + 1) * LANES) for hd in range(heads)]
    pvs, carries = _sb_chains(qs, [kn_ref[0, :, ln] for ln in pair_lanes], [vn_ref[0, :, ln] for ln in pair_lanes],
                              u_new, [jnp.zeros((tq, 1), F32)] * heads, [causal] * heads)
    for hd in range(heads):
        acc_ref[hd] = pvs[hd]
        carry_ref[hd] = carries[hd]

    def cond(state):
        return state[1] > 0

    def body(state):
        j = state[0]
        for cp in block_copies(j):
            cp.wait()
        kbs = [kbuf[p * LANES:(p + 1) * LANES, :].astype(BF16) for p in range(npair)]
        vbs = [vbuf[p * LANES:(p + 1) * LANES, :].astype(BF16) for p in range(npair)]
        pvs, new = _sb_chains(qs, [kbs[hd // 2] for hd in range(heads)], [vbs[hd // 2] for hd in range(heads)],
                              u, [carry_ref[hd] for hd in range(heads)], [None] * heads, keys_on_lanes=True)
        for hd in range(heads):
            acc_ref[hd] += pvs[hd]
            carry_ref[hd] = new[hd]
        more = jnp.logical_and(j > 0, _max_all(new) >= SB_DEAD_LOG)

        @pl.when(more)
        def _():
            for cp in block_copies(j - 1):
                cp.start()

        return (j - 1, more.astype(jnp.int32))

    alive = _max_all(carries) >= SB_DEAD_LOG

    @pl.when(jnp.logical_not(alive))
    def _():
        for cp in block_copies(npast - 1):
            cp.wait()

    lax.while_loop(cond, body, (jnp.int32(npast - 1), alive.astype(jnp.int32)))
    for p in range(npair):
        o_ref[0, :, p * LANES:(p + 1) * LANES] = jnp.where(first, acc_ref[2 * p], acc_ref[2 * p + 1]).astype(o_ref.dtype)


def _sb_sample(sb, cache_k, cache_v, *, layer, batch, seq, bw):
    depth, _, _, past = cache_k.shape
    heads = bw // SB_HEAD_DIM
    blk = min(SB_BLOCK, past)
    sb3 = sb.reshape(batch, seq, 3 * bw)
    vmem = (2 * 4 * _nbytes((seq, bw), BF16) + 2 * _nbytes((bw, blk), F32)
            + 2 * heads * _nbytes((seq, LANES), F32) + heads * 8 * _nbytes((seq, blk), F32))
    new = lambda col: pl.BlockSpec((1, seq, bw), lambda b: (b, 0, col))
    out = pl.pallas_call(
        functools.partial(_sb_sample_kernel, layer=layer, tq=seq, blk=blk, npast=past // blk),
        out_shape=jax.ShapeDtypeStruct((batch, seq, bw), BF16),
        grid=(batch,),
        in_specs=[new(0), new(1), new(2), pl.BlockSpec(memory_space=pl.ANY), pl.BlockSpec(memory_space=pl.ANY)],
        out_specs=new(0),
        scratch_shapes=[pltpu.VMEM((heads, seq, LANES), F32),
                        pltpu.VMEM((heads, seq, 1), F32),
                        pltpu.VMEM((bw, blk), F32),
                        pltpu.VMEM((bw, blk), F32),
                        pltpu.SemaphoreType.DMA((2,))],
        compiler_params=_cparams(1, vmem),
        name="sb_sample",
    )(sb3, sb3, sb3, cache_k, cache_v)
    return out.reshape(batch * seq, bw)


def _pool_kernel(*refs, tq, gdim, start, has_prev):
    if has_prev:
        u_ref, prev_ref, w_ref, sc_ref, o_ref, st_ref, ext_ref = refs
    else:
        u_ref, w_ref, sc_ref, o_ref, st_ref, ext_ref = refs
    t = pl.program_id(1)

    @pl.when(t == 0)
    def _():
        ext_ref[0:POOL_PAD, :] = jnp.zeros((POOL_PAD, ext_ref.shape[1]), F32)
        if has_prev:
            ext_ref[POOL_PAD - POOL_STATE:POOL_PAD, :] = prev_ref[0]

    @pl.when(t > 0)
    def _():
        ext_ref[0:POOL_PAD, :] = ext_ref[tq:tq + POOL_PAD, :]

    ext_ref[POOL_PAD:POOL_PAD + tq, :] = u_ref[0]
    pos = start + t * tq + lax.broadcasted_iota(jnp.int32, (tq, 1), 0)
    for g, w in enumerate(POOL_WINDOWS):
        cols = slice(g * gdim, (g + 1) * gdim)
        tok = ext_ref[POOL_PAD:POOL_PAD + tq, cols]
        s = tok
        for d in range(1, w):
            s = s + ext_ref[POOL_PAD - d:POOL_PAD - d + tq, cols]
        cnt = jnp.minimum(pos + 1, w).astype(F32)
        pooled = (s / cnt - tok).astype(BF16)
        o_ref[0, :, cols] = (_dot(pooled, w_ref[g]) * sc_ref[:, cols]).astype(o_ref.dtype)

    @pl.when(t == pl.num_programs(1) - 1)
    def _():
        st_ref[0] = ext_ref[POOL_PAD + tq - POOL_STATE:POOL_PAD + tq, :]


def _pool(u, prev, w_pool, scale, *, batch, seq, start):
    bw = u.shape[-1]
    tq = min(ROW_TILE, seq)
    gdim = bw // len(POOL_WINDOWS)
    has_prev = prev is not None
    u3 = u.reshape(batch, seq, bw)
    in_specs = [pl.BlockSpec((1, tq, bw), lambda b, t: (b, t, 0))]
    args = [u3]
    if has_prev:
        in_specs.append(pl.BlockSpec((1, POOL_STATE, bw), lambda b, t: (b, 0, 0)))
        args.append(prev)
    in_specs += [_const_spec(w_pool.shape), _const_spec(scale.shape)]
    args += [w_pool, scale]
    vmem = (2 * (_nbytes((tq, bw), F32) + _nbytes((tq, bw), BF16) + 2 * _nbytes((POOL_PAD, bw), F32))
            + _nbytes((POOL_PAD + tq, bw), F32) + _nbytes(w_pool.shape, BF16) + 6 * _nbytes((tq, bw), F32))
    o, st = pl.pallas_call(
        functools.partial(_pool_kernel, tq=tq, gdim=gdim, start=start, has_prev=has_prev),
        out_shape=(jax.ShapeDtypeStruct((batch, seq, bw), BF16),
                   jax.ShapeDtypeStruct((batch, POOL_STATE, bw), F32)),
        grid=(batch, seq // tq),
        in_specs=in_specs,
        out_specs=(pl.BlockSpec((1, tq, bw), lambda b, t: (b, t, 0)),
                   pl.BlockSpec((1, POOL_STATE, bw), lambda b, t: (b, 0, 0))),
        scratch_shapes=[pltpu.VMEM((POOL_PAD + tq, bw), F32)],
        compiler_params=_cparams(2, vmem),
        name="pool",
    )(*args)
    return o.reshape(batch * seq, bw), st


def _gla_kernel(*refs, tt, kw, vw, has_state):
    if has_state:
        gf_ref, vg_ref, s0_ref, gn_ref, o_ref, s_out_ref, s_ref, og_ref = refs
    else:
        gf_ref, vg_ref, gn_ref, o_ref, s_out_ref, s_ref, og_ref = refs
    t = pl.program_id(1)
    dk = kw // GLA_HEADS
    dv = vw // GLA_HEADS
    nh = GLA_HEADS

    @pl.when(t == 0)
    def _():
        if has_state:
            s_ref[...] = s0_ref[0]
        else:
            s_ref[...] = jnp.zeros(s_ref.shape, F32)

    r = lax.broadcasted_iota(jnp.int32, (CHUNK, CHUNK), 0)
    c = lax.broadcasted_iota(jnp.int32, (CHUNK, CHUNK), 1)
    tril = jnp.where(r >= c, 1.0, 0.0).astype(BF16)
    lane_head = lax.broadcasted_iota(jnp.int32, (CHUNK, kw), 1) // dk
    head_lanes = [lane_head == h for h in range(nh)]
    qi = lax.broadcasted_iota(jnp.int32, (nh * CHUNK, CHUNK), 0) % CHUNK
    kj = lax.broadcasted_iota(jnp.int32, (nh * CHUNK, CHUNK), 1)
    causal = qi >= kj

    def stack_heads(x):
        return jnp.concatenate([jnp.where(m, x, 0.0) for m in head_lanes], axis=0).astype(BF16)

    def diag(x, h, width):
        rows = x.shape[0] // nh
        return x[h * rows:(h + 1) * rows, h * width:(h + 1) * width]

    def prepare(ci):
        rows = slice(ci * CHUNK, (ci + 1) * CHUNK)
        q = gf_ref[0, rows, 0:kw] * (dk ** -0.5)
        k = gf_ref[0, rows, kw:2 * kw]
        hi, lo = _split_bf16(gf_ref[0, rows, 2 * kw:3 * kw])
        b = _dot(tril, hi) + _dot(tril, lo)
        bref = b[CHUNK // 2:CHUNK // 2 + 1, :]
        blast = b[CHUNK - 1:CHUNK, :]
        blast_col = jnp.broadcast_to(blast, (dv, kw)).T
        return dict(qe=stack_heads(q * jnp.exp(b - bref)), qb=stack_heads(q * jnp.exp(b)),
                    ke=(k * jnp.exp(bref - b)).astype(BF16), kl=(k * jnp.exp(blast - b)).astype(BF16),
                    decay=jnp.exp(blast_col))

    def scores(ci, p):
        v = vg_ref[0, ci * CHUNK:(ci + 1) * CHUNK, :]
        att = jnp.where(causal, _dot_nt(p.pop("qe"), p.pop("ke")), 0.0).astype(BF16)
        upd = _dot_tn(p.pop("kl"), v)
        p.update(att=att, upd=jnp.concatenate([diag(upd, h, dv) for h in range(nh)], axis=0))

    def finish(ci, p):
        rows = slice(ci * CHUNK, (ci + 1) * CHUNK)
        s_old = s_ref[...]
        inter = _dot(p["qb"], s_old.astype(BF16))
        for h in range(nh):
            hrows = slice(h * CHUNK, (h + 1) * CHUNK)
            intra = _dot(p["att"][hrows, :], vg_ref[0, rows, h * dv:(h + 1) * dv])
            og_ref[rows, h * dv:(h + 1) * dv] = intra + inter[hrows, :]
        s_ref[...] = s_old * p["decay"] + p["upd"]

    nchunk = tt // CHUNK
    work = {}
    for step in range(nchunk + 2):
        if step < nchunk:
            work[step] = prepare(step)
        if 0 <= step - 1 < nchunk:
            scores(step - 1, work[step - 1])
        if 0 <= step - 2 < nchunk:
            finish(step - 2, work.pop(step - 2))

    gn = gn_ref[...]
    for h in range(GLA_HEADS):
        cols = slice(h * dv, (h + 1) * dv)
        y = _rmsnorm(og_ref[:, cols], gn)
        rg = gf_ref[0, :, 3 * kw + h * dv:3 * kw + (h + 1) * dv]
        o_ref[0, :, cols] = (y * (rg / (1.0 + jnp.exp(-rg)))).astype(o_ref.dtype)

    @pl.when(t == pl.num_programs(1) - 1)
    def _():
        s_out_ref[0] = s_ref[...]


def _gla(gf, vg, s0, gnorm, *, batch, seq, kw, vw):
    tt = min(ROW_TILE, seq)
    dk, dv = kw // GLA_HEADS, vw // GLA_HEADS
    gfw = gf.shape[-1]
    has_state = s0 is not None
    in_specs = [pl.BlockSpec((1, tt, gfw), lambda b, t: (b, t, 0)),
                pl.BlockSpec((1, tt, vw), lambda b, t: (b, t, 0))]
    args = [gf.reshape(batch, seq, gfw), vg.reshape(batch, seq, vw)]
    state_spec = pl.BlockSpec((1, kw, dv), lambda b, t: (b, 0, 0))
    if has_state:
        in_specs.append(state_spec)
        args.append(s0.reshape(batch, kw, dv))
    in_specs.append(_const_spec(gnorm.shape))
    args.append(gnorm)
    vmem = (2 * (_nbytes((tt, gfw), F32) + 2 * _nbytes((tt, vw), BF16) + 2 * _nbytes((kw, dv), F32))
            + _nbytes((kw, dv), F32) + _nbytes((tt, vw), F32) + 16 * _nbytes((GLA_HEADS * CHUNK, vw), F32))
    o, st = pl.pallas_call(
        functools.partial(_gla_kernel, tt=tt, kw=kw, vw=vw, has_state=has_state),
        out_shape=(jax.ShapeDtypeStruct((batch, seq, vw), BF16),
                   jax.ShapeDtypeStruct((batch, kw, dv), F32)),
        grid=(batch, seq // tt),
        in_specs=in_specs,
        out_specs=(pl.BlockSpec((1, tt, vw), lambda b, t: (b, t, 0)), state_spec),
        scratch_shapes=[pltpu.VMEM((kw, dv), F32), pltpu.VMEM((tt, vw), F32)],
        compiler_params=_cparams(2, vmem),
        name="gla",
    )(*args)
    return o.reshape(batch * seq, vw), st.reshape(batch, GLA_HEADS, dk, dv)


def _merge_kernel(x_ref, g_ref, osb_ref, opool_ref, ogla_ref, wgate_ref, wbr_ref, wout_ref, o_ref, *, d):
    xf = x_ref[...]
    h = _rmsnorm(xf, g_ref[...]).astype(BF16)
    m = None
    for n, br_ref in enumerate((osb_ref, opool_ref, ogla_ref)):
        logits = _dot(h, wgate_ref[:, n * d:(n + 1) * d])
        gate = 1.0 / (1.0 + jnp.exp(-logits))
        term = gate * _dot(br_ref[...], wbr_ref[n])
        m = term if m is None else m + term
    o_ref[...] = xf + _dot(m.astype(BF16), wout_ref[...])


def _merge(x, g1, o_sb, o_pool, o_gla, wgate, wbr, wout):
    n, d = x.shape
    bw = o_sb.shape[-1]
    tm = min(ROW_TILE, n)
    row = lambda w: pl.BlockSpec((tm, w), lambda i: (i, 0))
    vmem = (2 * (2 * _nbytes((tm, d), F32) + 3 * _nbytes((tm, bw), BF16))
            + _nbytes(wgate.shape, BF16) + _nbytes(wbr.shape, BF16) + _nbytes(wout.shape, BF16)
            + 6 * _nbytes((tm, d), F32))
    return pl.pallas_call(
        functools.partial(_merge_kernel, d=d),
        out_shape=jax.ShapeDtypeStruct((n, d), F32),
        grid=(n // tm,),
        in_specs=[row(d), _const_spec(g1.shape), row(bw), row(bw), row(bw),
                  _const_spec(wgate.shape), _const_spec(wbr.shape), _const_spec(wout.shape)],
        out_specs=row(d),
        compiler_params=_cparams(1, vmem),
        name="merge",
    )(x, g1, o_sb, o_pool, o_gla, wgate, wbr, wout)


FFN_COLS = 256


def _gelu_tanh(x):
    c = 0.7978845608028654
    return 0.5 * x * (1.0 + jnp.tanh(c * (x + 0.044715 * (x * x * x))))


def _ffn_kernel(*refs, nseg, seg, dff, has_prev):
    if has_prev:
        x_ref, g_ref, prev_ref, wup_ref, cw_ref, cb_ref, wdn_ref, o_ref, st_ref, ext_ref, gact_ref, tail_ref = refs
    else:
        x_ref, g_ref, wup_ref, cw_ref, cb_ref, wdn_ref, o_ref, st_ref, ext_ref, gact_ref, tail_ref = refs
    t = pl.program_id(1)
    xf = x_ref[...]
    h = _rmsnorm(xf, g_ref[...]).astype(BF16)
    pad0 = CONV_PAD - (CONV_W - 1)

    if not has_prev:
        @pl.when(t == 0)
        def _():
            tail_ref[...] = jnp.zeros(tail_ref.shape, F32)

    def conv_half(c0):
        cols = slice(c0, c0 + FFN_COLS)
        up = _dot(h, wup_ref[:, cols])
        outs = []
        for s in range(nseg):
            if has_prev:
                ext_ref[s, pad0:CONV_PAD, :] = prev_ref[s, :, cols]
            else:
                ext_ref[s, 0:CONV_PAD, :] = tail_ref[:, cols]
            ext_ref[s, CONV_PAD:CONV_PAD + seg, :] = up[s * seg:(s + 1) * seg, :]
            acc = cb_ref[:, cols] + ext_ref[s, pad0:pad0 + seg, :] * cw_ref[0:1, cols]
            for i in range(1, CONV_W):
                acc = acc + ext_ref[s, pad0 + i:pad0 + i + seg, :] * cw_ref[i:i + 1, cols]
            outs.append(acc)
            if has_prev:
                st_ref[s, :, cols] = ext_ref[s, CONV_PAD + seg - (CONV_W - 1):CONV_PAD + seg, :]
            else:
                tail_ref[:, cols] = ext_ref[s, seg:seg + CONV_PAD, :]
        return outs[0] if nseg == 1 else jnp.concatenate(outs, axis=0)

    for c0 in range(0, dff, FFN_COLS):
        a = conv_half(c0)
        b = conv_half(dff + c0)
        gact_ref[:, c0:c0 + FFN_COLS] = (_gelu_tanh(a) * b).astype(BF16)

    o_ref[...] = xf + _dot(gact_ref[...], wdn_ref[...])

    if not has_prev:
        @pl.when(t == pl.num_programs(1) - 1)
        def _():
            st_ref[0] = tail_ref[CONV_PAD - (CONV_W - 1):CONV_PAD, :]


def _ffn(x, g2, prev, wup, cw, cb, wdn, *, batch, seq):
    n, d = x.shape
    dff = wdn.shape[0]
    has_prev = prev is not None
    if has_prev:
        nseg = max(1, min(ROW_TILE // seq, batch))
        seg = seq
        grid = (batch // nseg, 1)
        xmap = lambda i, t: (i, 0)
        stmap = lambda i, t: (i, 0, 0)
    else:
        nseg = 1
        seg = min(ROW_TILE, seq)
        grid = (batch, seq // seg)
        nt = seq // seg
        xmap = lambda b, t: (b * nt + t, 0)
        stmap = lambda b, t: (b, 0, 0)
    tm = nseg * seg
    in_specs = [pl.BlockSpec((tm, d), xmap), _const_spec(g2.shape)]
    args = [x, g2]
    if has_prev:
        in_specs.append(pl.BlockSpec((nseg, CONV_W - 1, 2 * dff), stmap))
        args.append(prev)
    in_specs += [_const_spec(wup.shape), _const_spec(cw.shape), _const_spec(cb.shape), _const_spec(wdn.shape)]
    args += [wup, cw, cb, wdn]
    vmem = (2 * (2 * _nbytes((tm, d), F32) + 2 * _nbytes((nseg, CONV_W - 1, 2 * dff), F32))
            + _nbytes(wup.shape, BF16) + _nbytes(wdn.shape, BF16) + 2 * _nbytes(cw.shape, F32)
            + _nbytes((nseg, CONV_PAD + seg, FFN_COLS), F32) + _nbytes((tm, dff), BF16)
            + _nbytes((CONV_PAD, 2 * dff), F32) + 4 * _nbytes((tm, d), F32) + 12 * _nbytes((tm, FFN_COLS), F32))
    o, st = pl.pallas_call(
        functools.partial(_ffn_kernel, nseg=nseg, seg=seg, dff=dff, has_prev=has_prev),
        out_shape=(jax.ShapeDtypeStruct((n, d), F32),
                   jax.ShapeDtypeStruct((batch, CONV_W - 1, 2 * dff), F32)),
        grid=grid,
        in_specs=in_specs,
        out_specs=(pl.BlockSpec((tm, d), xmap), pl.BlockSpec((nseg, CONV_W - 1, 2 * dff), stmap)),
        scratch_shapes=[pltpu.VMEM((nseg, CONV_PAD + seg, FFN_COLS), F32),
                        pltpu.VMEM((tm, dff), BF16),
                        pltpu.VMEM((CONV_PAD, 2 * dff), F32)],
        compiler_params=_cparams(2, vmem),
        name="ffn",
    )(*args)
    return o, st


def _final_norm_kernel(x_ref, g_ref, o_ref):
    o_ref[...] = _rmsnorm(x_ref[...], g_ref[...])


def _final_norm(x, g):
    n, d = x.shape
    tm = min(ROW_TILE, n)
    return pl.pallas_call(
        _final_norm_kernel,
        out_shape=jax.ShapeDtypeStruct((n, d), F32),
        grid=(n // tm,),
        in_specs=[pl.BlockSpec((tm, d), lambda i: (i, 0)), _const_spec(g.shape)],
        out_specs=pl.BlockSpec((tm, d), lambda i: (i, 0)),
        compiler_params=_cparams(1, 6 * _nbytes((tm, d), F32)),
        name="final_norm",
    )(x, g)


def _layer(x, w, kstack, vstack, *, layer, depth, batch, seq, start, cache_k, cache_v, prev_pool, prev_gla,
           prev_conv):
    d = x.shape[-1]
    bw, kw = d // 2, d // 4
    sb, kstack, vstack, u, gf, vg = _inproj(x, w["g1"], w["wa"], w["wl"], w["wg"], w["bg"], kstack, vstack,
                                            layer=layer, depth=depth, bw=bw, kw=kw)
    if cache_k is None:
        o_sb = _sb_prompt(sb, batch=batch, seq=seq, bw=bw)
    else:
        o_sb = _sb_sample(sb, cache_k, cache_v, layer=layer, batch=batch, seq=seq, bw=bw)
    o_pool, pool_state = _pool(u, prev_pool, w["w_pool"], w["pool_scale"], batch=batch, seq=seq, start=start)
    o_gla, gla_state = _gla(gf, vg, prev_gla, w["gnorm"], batch=batch, seq=seq, kw=kw, vw=bw)
    x = _merge(x, w["g1"], o_sb, o_pool, o_gla, w["wgate"], w["wbr"], w["wout"])
    x, conv_state = _ffn(x, w["g2"], prev_conv, w["wup"], w["cw"], w["cb"], w["wdn"], batch=batch, seq=seq)
    return x, kstack, vstack, (pool_state, gla_state, conv_state)


def kernel(x_prompt, x_sample, cache_k, cache_v, state_pool, state_gla, state_conv, norm1_g, w_in, w_gla_gate,
           b_gla_gate, gla_norm_g, w_pool, pool_scale, w_branch, w_out, norm2_g, w_up, conv_w, conv_b, w_down,
           final_g):
    bp, tp, d = x_prompt.shape
    bs, ts, _ = x_sample.shape
    depth = w_in.shape[0]
    past = cache_k.shape[2]
    bw, kw = d // 2, d // 4
    rank = w_gla_gate.shape[1]
    n_main = 4 * bw + 2 * kw + 2 * bw
    xp = x_prompt.reshape(bp * tp, d)
    xs = x_sample.reshape(bs * ts, d)
    outs_p, outs_s = [], []
    kp = vp = ks = vs = None
    cache_kt = jnp.transpose(cache_k, (0, 1, 3, 4, 2)).reshape(depth, bs, bw, past)
    cache_vt = jnp.transpose(cache_v, (0, 1, 3, 4, 2)).reshape(depth, bs, bw, past)
    for l in range(depth):
        wl_pad = jnp.zeros((d, LANES), F32).at[:, :rank].set(w_in[l, :, n_main:n_main + rank])
        wg_pad = jnp.zeros((LANES, kw), F32).at[:rank, :].set(w_gla_gate[l])
        w = dict(
            g1=norm1_g[l].reshape(1, d), g2=norm2_g[l].reshape(1, d),
            wa=w_in[l, :, :n_main].astype(BF16), wl=wl_pad.astype(BF16), wg=wg_pad.astype(BF16),
            bg=b_gla_gate[l].reshape(1, kw),
            wgate=w_in[l, :, n_main + rank:].astype(BF16),
            w_pool=w_pool[l].astype(BF16), pool_scale=pool_scale[l].reshape(1, bw),
            gnorm=gla_norm_g[l].reshape(1, -1),
            wbr=w_branch[l].astype(BF16), wout=w_out[l].astype(BF16),
            wup=w_up[l].astype(BF16), cw=conv_w[l], cb=conv_b[l].reshape(1, -1), wdn=w_down[l].astype(BF16),
        )
        xp, kp, vp, st = _layer(xp, w, kp, vp, layer=l, depth=depth, batch=bp, seq=tp, start=0,
                                cache_k=None, cache_v=None, prev_pool=None, prev_gla=None, prev_conv=None)
        outs_p.append(st)
        xs, ks, vs, st = _layer(xs, w, ks, vs, layer=l, depth=depth, batch=bs, seq=ts, start=past,
                                cache_k=cache_kt, cache_v=cache_vt, prev_pool=state_pool[l], prev_gla=state_gla[l],
                                prev_conv=state_conv[l])
        outs_s.append(st)
    g = final_g.reshape(1, d)
    y_prompt = _final_norm(xp, g).reshape(bp, tp, d)
    y_sample = _final_norm(xs, g).reshape(bs, ts, d)
    stack = lambda outs, i: jnp.stack([o[i] for o in outs])
    heads = bw // SB_HEAD_DIM
    kv_p = (depth, bp, tp, heads, SB_HEAD_DIM)
    kv_s = (depth, bs, ts, heads, SB_HEAD_DIM)
    return (y_prompt, y_sample,
            kp.reshape(kv_p), vp.reshape(kv_p), stack(outs_p, 0), stack(outs_p, 1), stack(outs_p, 2),
            ks.reshape(kv_s), vs.reshape(kv_s), stack(outs_s, 0), stack(outs_s, 1), stack(outs_s, 2))
```

```python
import functools

import jax
import jax.numpy as jnp
from jax import lax
from jax.experimental import pallas as pl
from jax.experimental.pallas import tpu as pltpu

F32 = jnp.float32
BF16 = jnp.bfloat16

NORM_EPS = 1e-6
LOG2E = 1.4426950408889634
CHUNK = 64
SB_HEAD_DIM = 64
POOL_WINDOWS = (2, 4, 8, 16)
POOL_STATE = max(POOL_WINDOWS) - 1
POOL_PAD = 32
GLA_HEADS = 4
GLA_GATE_TAU = 16.0
CONV_W = 3
CONV_PAD = 8
LANES = 128
V7X_SCOPED_VMEM_BYTES = 60000 * 1024

ROW_TILE = 512
SB_BLOCK = 256
SB_PAIRS = 4
SB_DEAD_LOG = -150.0


def _cparams(n_axes, vmem_bytes):
    return pltpu.CompilerParams(
        dimension_semantics=("arbitrary",) * n_axes,
        vmem_limit_bytes=int(min(vmem_bytes, V7X_SCOPED_VMEM_BYTES)),
    )


def _nbytes(shape, dtype):
    n = 1
    for s in shape:
        n *= s
    return n * jnp.dtype(dtype).itemsize


def _const_spec(shape):
    nd = len(shape)
    return pl.BlockSpec(shape, lambda *_: (0,) * nd, pipeline_mode=pl.Buffered(1))


def _rmsnorm(xf, g):
    ms = jnp.mean(xf * xf, axis=-1, keepdims=True)
    return (xf * lax.rsqrt(ms + NORM_EPS)) * g


def _log_sigmoid(x):
    return jnp.minimum(x, 0.0) - jnp.log1p(jnp.exp(-jnp.abs(x)))


def _split_bf16(x):
    hi = x.astype(BF16)
    lo = (x - hi.astype(F32)).astype(BF16)
    return hi, lo


def _dot(a, b):
    return jnp.dot(a, b, preferred_element_type=F32)


def _dot_nt(a, b):
    return lax.dot_general(a, b, (((1,), (1,)), ((), ())), preferred_element_type=F32)


def _dot_tn(a, b):
    return lax.dot_general(a, b, (((0,), (0,)), ((), ())), preferred_element_type=F32)


def _inproj_kernel(*refs, bw, kw, aliased):
    x_ref, g_ref, wa_ref, wl_ref, wg_ref, bg_ref = refs[:6]
    sb_ref, k_ref, v_ref, u_ref, gf_ref, vg_ref = refs[6 + 2 * aliased:]
    h = _rmsnorm(x_ref[...], g_ref[...]).astype(BF16)
    tm = x_ref.shape[0]
    heads = bw // SB_HEAD_DIM

    def proj(c0, c1):
        return _dot(h, wa_ref[:, c0:c1])

    def store_heads(dst_ref, val):
        for hd in range(heads):
            dst_ref[pl.ds(hd, tm, stride=heads), :] = val[:, hd * SB_HEAD_DIM:(hd + 1) * SB_HEAD_DIM]

    sb_ref[:, 0:bw] = (proj(0, bw) * (SB_HEAD_DIM ** -0.5 * LOG2E)).astype(BF16)
    k = proj(bw, 2 * bw)
    store_heads(k_ref, k)
    sb_ref[:, bw:2 * bw] = k.astype(BF16)
    v = proj(2 * bw, 3 * bw)
    store_heads(v_ref, v)
    sb_ref[:, 2 * bw:3 * bw] = v.astype(BF16)
    u_ref[...] = proj(3 * bw, 4 * bw)
    o = 4 * bw
    gf_ref[:, 0:kw] = proj(o, o + kw)
    gf_ref[:, kw:2 * kw] = proj(o + kw, o + 2 * kw)
    vg_ref[...] = proj(o + 2 * kw, o + 2 * kw + bw).astype(BF16)
    gf_ref[:, 3 * kw:3 * kw + bw] = proj(o + 2 * kw + bw, o + 2 * kw + 2 * bw)
    g_lr = _dot(h, wl_ref[...]).astype(BF16)
    pre = _dot(g_lr, wg_ref[...]) + bg_ref[...]
    gf_ref[:, 2 * kw:3 * kw] = _log_sigmoid(pre) * (1.0 / GLA_GATE_TAU)


def _inproj(x, g1, wa, wl, wg, bg, kstack, vstack, *, layer, depth, bw, kw):
    n, d = x.shape
    tm = min(ROW_TILE, n)
    gfw = 3 * kw + bw
    heads = bw // SB_HEAD_DIM
    aliased = kstack is not None
    kv_shape = jax.ShapeDtypeStruct((depth, n * heads, SB_HEAD_DIM), F32)
    out_shape = (
        jax.ShapeDtypeStruct((n, 3 * bw), BF16),
        kv_shape,
        kv_shape,
        jax.ShapeDtypeStruct((n, bw), F32),
        jax.ShapeDtypeStruct((n, gfw), F32),
        jax.ShapeDtypeStruct((n, bw), BF16),
    )
    row = lambda w: pl.BlockSpec((tm, w), lambda i: (i, 0))
    slab = pl.BlockSpec((None, tm * heads, SB_HEAD_DIM), lambda i: (layer, i, 0))
    blocks = 2 * (_nbytes((tm, d), F32) + _nbytes((tm, 3 * bw), BF16) + 2 * _nbytes((tm, heads, LANES), F32)
                  + _nbytes((tm, bw), F32) + _nbytes((tm, gfw), F32) + _nbytes((tm, bw), BF16))
    weights = _nbytes(wa.shape, BF16) + _nbytes(wl.shape, BF16) + _nbytes(wg.shape, BF16)
    temps = 4 * _nbytes((tm, d), F32)
    in_specs = [row(d), _const_spec(g1.shape), _const_spec(wa.shape), _const_spec(wl.shape),
                _const_spec(wg.shape), _const_spec(bg.shape)]
    args = [x, g1, wa, wl, wg, bg]
    if aliased:
        in_specs += [pl.BlockSpec(memory_space=pl.ANY)] * 2
        args += [kstack, vstack]
    return pl.pallas_call(
        functools.partial(_inproj_kernel, bw=bw, kw=kw, aliased=aliased),
        out_shape=out_shape,
        grid=(n // tm,),
        in_specs=in_specs,
        out_specs=(row(3 * bw), slab, slab, row(bw), row(gfw), row(bw)),
        input_output_aliases={6: 1, 7: 2} if aliased else {},
        compiler_params=_cparams(1, blocks + weights + temps),
        name="inproj",
    )(*args)


def _suffix_sum_matrix(n):
    r = lax.broadcasted_iota(jnp.int32, (n, n), 0)
    c = lax.broadcasted_iota(jnp.int32, (n, n), 1)
    return jnp.where(r > c, 1.0, 0.0).astype(BF16)


def _sb_chains(qs, kbs, vbs, u, carries, masks, keys_on_lanes=False):
    n = len(qs)
    sign = jnp.uint32(0x80000000)
    ls_pos, ls_neg, ws, new, pvs = {}, {}, {}, [], []
    for t in range(n + 2):
        if t < n:
            s = _dot(qs[t], kbs[t]) if keys_on_lanes else _dot_nt(qs[t], kbs[t])
            nabs = pltpu.bitcast(pltpu.bitcast(s, jnp.uint32) | sign, F32)
            sp = jnp.log(1.0 + jnp.exp2(nabs)) * LOG2E
            pos = jnp.minimum(s, 0.0) - sp
            neg = pos - s
            ls_pos[t] = pos
            ls_neg[t] = neg if masks[t] is None else jnp.where(masks[t], neg, 0.0)
        i = t - 1
        if 0 <= i < n:
            carry = new[carries[i]] if isinstance(carries[i], int) else carries[i]
            later = _dot(ls_neg[i].astype(BF16), u)
            w = jnp.exp2(ls_pos.pop(i) + later + carry)
            ws[i] = (w if masks[i] is None else jnp.where(masks[i], w, 0.0)).astype(BF16)
            new.append(carry + jnp.sum(ls_neg.pop(i), axis=-1, keepdims=True))
        i = t - 2
        if 0 <= i < n:
            pvs.append(_dot_nt(ws.pop(i), vbs[i]) if keys_on_lanes else _dot(ws.pop(i), vbs[i]))
    return pvs, new


def _head_masks():
    lane = lax.broadcasted_iota(jnp.int32, (1, LANES), 1)
    return lane < SB_HEAD_DIM


def _max_all(carries):
    m = carries[0]
    for c in carries[1:]:
        m = jnp.maximum(m, c)
    return jnp.max(m)


def _sb_prompt_kernel(q_ref, k_ref, v_ref, o_ref, acc_ref, carry_ref, *, blk):
    i = pl.program_id(2)
    npair = q_ref.shape[-1] // LANES
    heads = 2 * npair
    first = _head_masks()
    u = _suffix_sum_matrix(blk)
    r = lax.broadcasted_iota(jnp.int32, (blk, blk), 0)
    c = lax.broadcasted_iota(jnp.int32, (blk, blk), 1)
    causal = c < r
    qs = []
    for p in range(npair):
        q = q_ref[0, :, p * LANES:(p + 1) * LANES]
        qs += [jnp.where(first, q, jnp.zeros_like(q)), jnp.where(first, jnp.zeros_like(q), q)]

    def operands(ref, j):
        off = pl.multiple_of(j * blk, blk)
        blocks = [ref[0, pl.ds(off, blk), p * LANES:(p + 1) * LANES] for p in range(npair)]
        return [blocks[hd // 2] for hd in range(heads)]

    def sweep(j):
        pvs, new = _sb_chains(qs, operands(k_ref, j), operands(v_ref, j), u,
                              [carry_ref[hd] for hd in range(heads)], [None] * heads)
        for hd in range(heads):
            carry_ref[hd] = new[hd]
            acc_ref[hd] += pvs[hd]
        return _max_all(new)

    zero = jnp.zeros((blk, 1), F32)

    @pl.when(i == 0)
    def _():
        pvs, _ = _sb_chains(qs, operands(k_ref, 0), operands(v_ref, 0), u, [zero] * heads, [causal] * heads)
        for hd in range(heads):
            acc_ref[hd] = pvs[hd]

    @pl.when(i > 0)
    def _():
        pvs, new = _sb_chains(qs * 2, operands(k_ref, i) + operands(k_ref, i - 1),
                              operands(v_ref, i) + operands(v_ref, i - 1), u,
                              [zero] * heads + list(range(heads)), [causal] * heads + [None] * heads)
        for hd in range(heads):
            acc_ref[hd] = pvs[hd] + pvs[heads + hd]
            carry_ref[hd] = new[heads + hd]

        def cond(state):
            return jnp.logical_and(state[0] >= 0, state[1] >= SB_DEAD_LOG)

        def body(state):
            return (state[0] - 1, sweep(state[0]))

        lax.while_loop(cond, body, (i - 2, _max_all(new[heads:])))

    for p in range(npair):
        o_ref[0, :, p * LANES:(p + 1) * LANES] = jnp.where(first, acc_ref[2 * p], acc_ref[2 * p + 1]).astype(o_ref.dtype)


def _sb_prompt(sb, *, batch, seq, bw):
    blk = min(SB_BLOCK, seq)
    nq = seq // blk
    width = SB_PAIRS * LANES
    ngrp = bw // width
    sb3 = sb.reshape(batch, seq, 3 * bw)
    vmem = (2 * (2 * _nbytes((blk, width), BF16) + 2 * _nbytes((seq, width), BF16))
            + 4 * SB_PAIRS * _nbytes((blk, LANES), F32) + 2 * SB_PAIRS * 8 * _nbytes((blk, blk), F32))
    out = pl.pallas_call(
        functools.partial(_sb_prompt_kernel, blk=blk),
        out_shape=jax.ShapeDtypeStruct((batch, seq, bw), BF16),
        grid=(batch, ngrp, nq),
        in_specs=[
            pl.BlockSpec((1, blk, width), lambda b, p, i: (b, i, p)),
            pl.BlockSpec((1, seq, width), lambda b, p, i: (b, 0, ngrp + p)),
            pl.BlockSpec((1, seq, width), lambda b, p, i: (b, 0, 2 * ngrp + p)),
        ],
        out_specs=pl.BlockSpec((1, blk, width), lambda b, p, i: (b, i, p)),
        scratch_shapes=[pltpu.VMEM((2 * SB_PAIRS, blk, LANES), F32), pltpu.VMEM((2 * SB_PAIRS, blk, 1), F32)],
        compiler_params=_cparams(3, vmem),
        name="sb_prompt",
    )(sb3, sb3, sb3)
    return out.reshape(batch * seq, bw)


def _sb_sample_kernel(q_ref, kn_ref, vn_ref, kp_hbm, vp_hbm, o_ref, acc_ref, carry_ref, kbuf, vbuf, sem, *,
                      layer, tq, blk, npast):
    b = pl.program_id(0)
    heads = acc_ref.shape[0]
    npair = heads // 2
    first = _head_masks()
    u_new = _suffix_sum_matrix(tq)
    u = _suffix_sum_matrix(blk)
    r = lax.broadcasted_iota(jnp.int32, (tq, tq), 0)
    c = lax.broadcasted_iota(jnp.int32, (tq, tq), 1)
    causal = c < r

    def block_copies(j):
        rows = pl.ds(pl.multiple_of(j * blk, blk), blk)
        return (pltpu.make_async_copy(kp_hbm.at[layer, b, :, rows], kbuf, sem.at[0]),
                pltpu.make_async_copy(vp_hbm.at[layer, b, :, rows], vbuf, sem.at[1]))

    for cp in block_copies(npast - 1):
        cp.start()

    qs = []
    for p in range(npair):
        q = q_ref[0, :, p * LANES:(p + 1) * LANES]
        qs += [jnp.where(first, q, jnp.zeros_like(q)), jnp.where(first, jnp.zeros_like(q), q)]

    pair_lanes = [slice((hd // 2) * LANES, (hd // 2 + 1) * LANES) for hd in range(heads)]
    pvs, carries = _sb_chains(qs, [kn_ref[0, :, ln] for ln in pair_lanes], [vn_ref[0, :, ln] for ln in pair_lanes],
                              u_new, [jnp.zeros((tq, 1), F32)] * heads, [causal] * heads)
    for hd in range(heads):
        acc_ref[hd] = pvs[hd]
        carry_ref[hd] = carries[hd]

    def cond(state):
        return state[1] > 0

    def body(state):
        j = state[0]
        for cp in block_copies(j):
            cp.wait()
        kbs = [kbuf[p * LANES:(p + 1) * LANES, :].astype(BF16) for p in range(npair)]
        vbs = [vbuf[p * LANES:(p + 1) * LANES, :].astype(BF16) for p in range(npair)]
        pvs, new = _sb_chains(qs, [kbs[hd // 2] for hd in range(heads)], [vbs[hd // 2] for hd in range(heads)],
                              u, [carry_ref[hd] for hd in range(heads)], [None] * heads, keys_on_lanes=True)
        for hd in range(heads):
            acc_ref[hd] += pvs[hd]
            carry_ref[hd] = new[hd]
        more = jnp.logical_and(j > 0, _max_all(new) >= SB_DEAD_LOG)

        @pl.when(more)
        def _():
            for cp in block_copies(j - 1):
                cp.start()

        return (j - 1, more.astype(jnp.int32))

    alive = _max_all(carries) >= SB_DEAD_LOG

    @pl.when(jnp.logical_not(alive))
    def _():
        for cp in block_copies(npast - 1):
            cp.wait()

    lax.while_loop(cond, body, (jnp.int32(npast - 1), alive.astype(jnp.int32)))
    for p in range(npair):
        o_ref[0, :, p * LANES:(p + 1) * LANES] = jnp.where(first, acc_ref[2 * p], acc_ref[2 * p + 1]).astype(o_ref.dtype)


def _sb_sample(sb, cache_k, cache_v, *, layer, batch, seq, bw):
    depth, _, _, past = cache_k.shape
    heads = bw // SB_HEAD_DIM
    blk = min(SB_BLOCK, past)
    sb3 = sb.reshape(batch, seq, 3 * bw)
    vmem = (2 * 4 * _nbytes((seq, bw), BF16) + 2 * _nbytes((bw, blk), F32)
            + 2 * heads * _nbytes((seq, LANES), F32) + heads * 8 * _nbytes((seq, blk), F32))
    new = lambda col: pl.BlockSpec((1, seq, bw), lambda b: (b, 0, col))
    out = pl.pallas_call(
        functools.partial(_sb_sample_kernel, layer=layer, tq=seq, blk=blk, npast=past // blk),
        out_shape=jax.ShapeDtypeStruct((batch, seq, bw), BF16),
        grid=(batch,),
        in_specs=[new(0), new(1), new(2), pl.BlockSpec(memory_space=pl.ANY), pl.BlockSpec(memory_space=pl.ANY)],
        out_specs=new(0),
        scratch_shapes=[pltpu.VMEM((heads, seq, LANES), F32),
                        pltpu.VMEM((heads, seq, 1), F32),
                        pltpu.VMEM((bw, blk), F32),
                        pltpu.VMEM((bw, blk), F32),
                        pltpu.SemaphoreType.DMA((2,))],
        compiler_params=_cparams(1, vmem),
        name="sb_sample",
    )(sb3, sb3, sb3, cache_k, cache_v)
    return out.reshape(batch * seq, bw)


def _pool_kernel(*refs, tq, gdim, start, has_prev):
    if has_prev:
        u_ref, prev_ref, w_ref, sc_ref, o_ref, st_ref, ext_ref, lvl_ref = refs
    else:
        u_ref, w_ref, sc_ref, o_ref, st_ref, ext_ref, lvl_ref = refs
    t = pl.program_id(1)

    @pl.when(t == 0)
    def _():
        ext_ref[0:POOL_PAD, :] = jnp.zeros((POOL_PAD, ext_ref.shape[1]), F32)
        if has_prev:
            ext_ref[POOL_PAD - POOL_STATE:POOL_PAD, :] = prev_ref[0]

    @pl.when(t > 0)
    def _():
        ext_ref[0:POOL_PAD, :] = ext_ref[tq:tq + POOL_PAD, :]

    ext_ref[POOL_PAD:POOL_PAD + tq, :] = u_ref[0]
    pos = start + t * tq + lax.broadcasted_iota(jnp.int32, (tq, 1), 0)
    nrow = POOL_PAD + tq
    for g, w in enumerate(POOL_WINDOWS):
        cols = slice(g * gdim, (g + 1) * gdim)
        tok = ext_ref[POOL_PAD:nrow, cols]
        def level(slot, r0, r1):
            return ext_ref[r0:r1, cols] if slot is None else lvl_ref[slot, r0:r1, :]

        slot, lo, m = None, 0, 1
        while 2 * m < w:
            lo += 8
            nxt = 0 if slot != 0 else 1
            lvl_ref[nxt, lo:nrow, :] = level(slot, lo, nrow) + level(slot, lo - m, nrow - m)
            slot, m = nxt, 2 * m
        s = level(slot, POOL_PAD, nrow) + level(slot, POOL_PAD - m, nrow - m)
        cnt = jnp.minimum(pos + 1, w).astype(F32)
        pooled = (s / cnt - tok).astype(BF16)
        o_ref[0, :, cols] = (_dot(pooled, w_ref[g]) * sc_ref[:, cols]).astype(o_ref.dtype)

    @pl.when(t == pl.num_programs(1) - 1)
    def _():
        st_ref[0] = ext_ref[POOL_PAD + tq - POOL_STATE:POOL_PAD + tq, :]


def _pool(u, prev, w_pool, scale, *, batch, seq, start):
    bw = u.shape[-1]
    tq = min(ROW_TILE, seq)
    gdim = bw // len(POOL_WINDOWS)
    has_prev = prev is not None
    u3 = u.reshape(batch, seq, bw)
    in_specs = [pl.BlockSpec((1, tq, bw), lambda b, t: (b, t, 0))]
    args = [u3]
    if has_prev:
        in_specs.append(pl.BlockSpec((1, POOL_STATE, bw), lambda b, t: (b, 0, 0)))
        args.append(prev)
    in_specs += [_const_spec(w_pool.shape), _const_spec(scale.shape)]
    args += [w_pool, scale]
    vmem = (2 * (_nbytes((tq, bw), F32) + _nbytes((tq, bw), BF16) + 2 * _nbytes((POOL_PAD, bw), F32))
            + _nbytes((POOL_PAD + tq, bw), F32) + 2 * _nbytes((POOL_PAD + tq, gdim), F32)
            + _nbytes(w_pool.shape, BF16) + 6 * _nbytes((tq, bw), F32))
    o, st = pl.pallas_call(
        functools.partial(_pool_kernel, tq=tq, gdim=gdim, start=start, has_prev=has_prev),
        out_shape=(jax.ShapeDtypeStruct((batch, seq, bw), BF16),
                   jax.ShapeDtypeStruct((batch, POOL_STATE, bw), F32)),
        grid=(batch, seq // tq),
        in_specs=in_specs,
        out_specs=(pl.BlockSpec((1, tq, bw), lambda b, t: (b, t, 0)),
                   pl.BlockSpec((1, POOL_STATE, bw), lambda b, t: (b, 0, 0))),
        scratch_shapes=[pltpu.VMEM((POOL_PAD + tq, bw), F32), pltpu.VMEM((2, POOL_PAD + tq, gdim), F32)],
        compiler_params=_cparams(2, vmem),
        name="pool",
    )(*args)
    return o.reshape(batch * seq, bw), st


def _gla_kernel(*refs, tt, kw, vw, has_state):
    if has_state:
        gf_ref, vg_ref, s0_ref, gn_ref, o_ref, s_out_ref, s_ref, og_ref = refs
    else:
        gf_ref, vg_ref, gn_ref, o_ref, s_out_ref, s_ref, og_ref = refs
    t = pl.program_id(1)
    dk = kw // GLA_HEADS
    dv = vw // GLA_HEADS
    nh = GLA_HEADS

    @pl.when(t == 0)
    def _():
        if has_state:
            s_ref[...] = s0_ref[0]
        else:
            s_ref[...] = jnp.zeros(s_ref.shape, F32)

    r = lax.broadcasted_iota(jnp.int32, (CHUNK, CHUNK), 0)
    c = lax.broadcasted_iota(jnp.int32, (CHUNK, CHUNK), 1)
    tril = jnp.where(r >= c, 1.0, 0.0).astype(BF16)
    lane_head = lax.broadcasted_iota(jnp.int32, (CHUNK, kw), 1) // dk
    head_lanes = [lane_head == h for h in range(nh)]
    qi = lax.broadcasted_iota(jnp.int32, (nh * CHUNK, CHUNK), 0) % CHUNK
    kj = lax.broadcasted_iota(jnp.int32, (nh * CHUNK, CHUNK), 1)
    causal = qi >= kj

    def stack_heads(x):
        return jnp.concatenate([jnp.where(m, x, 0.0) for m in head_lanes], axis=0).astype(BF16)

    def diag(x, h, width):
        rows = x.shape[0] // nh
        return x[h * rows:(h + 1) * rows, h * width:(h + 1) * width]

    def prepare(ci):
        rows = slice(ci * CHUNK, (ci + 1) * CHUNK)
        q = gf_ref[0, rows, 0:kw] * (dk ** -0.5)
        k = gf_ref[0, rows, kw:2 * kw]
        hi, lo = _split_bf16(gf_ref[0, rows, 2 * kw:3 * kw])
        b = _dot(tril, hi) + _dot(tril, lo)
        bref = b[CHUNK // 2:CHUNK // 2 + 1, :]
        blast = b[CHUNK - 1:CHUNK, :]
        blast_col = jnp.broadcast_to(blast, (dv, kw)).T
        return dict(qe=stack_heads(q * jnp.exp(b - bref)), qb=stack_heads(q * jnp.exp(b)),
                    ke=(k * jnp.exp(bref - b)).astype(BF16), kl=(k * jnp.exp(blast - b)).astype(BF16),
                    decay=jnp.exp(blast_col))

    def scores(ci, p):
        v = vg_ref[0, ci * CHUNK:(ci + 1) * CHUNK, :]
        att = jnp.where(causal, _dot_nt(p.pop("qe"), p.pop("ke")), 0.0).astype(BF16)
        upd = _dot_tn(p.pop("kl"), v)
        p.update(att=att, upd=jnp.concatenate([diag(upd, h, dv) for h in range(nh)], axis=0))

    def finish(ci, p):
        rows = slice(ci * CHUNK, (ci + 1) * CHUNK)
        s_old = s_ref[...]
        inter = _dot(p["qb"], s_old.astype(BF16))
        for h in range(nh):
            hrows = slice(h * CHUNK, (h + 1) * CHUNK)
            intra = _dot(p["att"][hrows, :], vg_ref[0, rows, h * dv:(h + 1) * dv])
            og_ref[rows, h * dv:(h + 1) * dv] = intra + inter[hrows, :]
        s_ref[...] = s_old * p["decay"] + p["upd"]

    nchunk = tt // CHUNK
    work = {}
    for step in range(nchunk + 2):
        if step < nchunk:
            work[step] = prepare(step)
        if 0 <= step - 1 < nchunk:
            scores(step - 1, work[step - 1])
        if 0 <= step - 2 < nchunk:
            finish(step - 2, work.pop(step - 2))

    gn = gn_ref[...]
    for h in range(GLA_HEADS):
        cols = slice(h * dv, (h + 1) * dv)
        y = _rmsnorm(og_ref[:, cols], gn)
        rg = gf_ref[0, :, 3 * kw + h * dv:3 * kw + (h + 1) * dv]
        o_ref[0, :, cols] = (y * (rg / (1.0 + jnp.exp(-rg)))).astype(o_ref.dtype)

    @pl.when(t == pl.num_programs(1) - 1)
    def _():
        s_out_ref[0] = s_ref[...]


def _gla(gf, vg, s0, gnorm, *, batch, seq, kw, vw):
    tt = min(ROW_TILE, seq)
    dk, dv = kw // GLA_HEADS, vw // GLA_HEADS
    gfw = gf.shape[-1]
    has_state = s0 is not None
    in_specs = [pl.BlockSpec((1, tt, gfw), lambda b, t: (b, t, 0)),
                pl.BlockSpec((1, tt, vw), lambda b, t: (b, t, 0))]
    args = [gf.reshape(batch, seq, gfw), vg.reshape(batch, seq, vw)]
    state_spec = pl.BlockSpec((1, kw, dv), lambda b, t: (b, 0, 0))
    if has_state:
        in_specs.append(state_spec)
        args.append(s0.reshape(batch, kw, dv))
    in_specs.append(_const_spec(gnorm.shape))
    args.append(gnorm)
    vmem = (2 * (_nbytes((tt, gfw), F32) + 2 * _nbytes((tt, vw), BF16) + 2 * _nbytes((kw, dv), F32))
            + _nbytes((kw, dv), F32) + _nbytes((tt, vw), F32) + 16 * _nbytes((GLA_HEADS * CHUNK, vw), F32))
    o, st = pl.pallas_call(
        functools.partial(_gla_kernel, tt=tt, kw=kw, vw=vw, has_state=has_state),
        out_shape=(jax.ShapeDtypeStruct((batch, seq, vw), BF16),
                   jax.ShapeDtypeStruct((batch, kw, dv), F32)),
        grid=(batch, seq // tt),
        in_specs=in_specs,
        out_specs=(pl.BlockSpec((1, tt, vw), lambda b, t: (b, t, 0)), state_spec),
        scratch_shapes=[pltpu.VMEM((kw, dv), F32), pltpu.VMEM((tt, vw), F32)],
        compiler_params=_cparams(2, vmem),
        name="gla",
    )(*args)
    return o.reshape(batch * seq, vw), st.reshape(batch, GLA_HEADS, dk, dv)


def _merge_kernel(x_ref, g_ref, osb_ref, opool_ref, ogla_ref, wgate_ref, wbr_ref, wout_ref, o_ref, *, d):
    xf = x_ref[...]
    h = _rmsnorm(xf, g_ref[...]).astype(BF16)
    m = None
    for n, br_ref in enumerate((osb_ref, opool_ref, ogla_ref)):
        logits = _dot(h, wgate_ref[:, n * d:(n + 1) * d])
        gate = 1.0 / (1.0 + jnp.exp(-logits))
        term = gate * _dot(br_ref[...], wbr_ref[n])
        m = term if m is None else m + term
    o_ref[...] = xf + _dot(m.astype(BF16), wout_ref[...])


def _merge(x, g1, o_sb, o_pool, o_gla, wgate, wbr, wout):
    n, d = x.shape
    bw = o_sb.shape[-1]
    tm = min(ROW_TILE, n)
    row = lambda w: pl.BlockSpec((tm, w), lambda i: (i, 0))
    vmem = (2 * (2 * _nbytes((tm, d), F32) + 3 * _nbytes((tm, bw), BF16))
            + _nbytes(wgate.shape, BF16) + _nbytes(wbr.shape, BF16) + _nbytes(wout.shape, BF16)
            + 6 * _nbytes((tm, d), F32))
    return pl.pallas_call(
        functools.partial(_merge_kernel, d=d),
        out_shape=jax.ShapeDtypeStruct((n, d), F32),
        grid=(n // tm,),
        in_specs=[row(d), _const_spec(g1.shape), row(bw), row(bw), row(bw),
                  _const_spec(wgate.shape), _const_spec(wbr.shape), _const_spec(wout.shape)],
        out_specs=row(d),
        compiler_params=_cparams(1, vmem),
        name="merge",
    )(x, g1, o_sb, o_pool, o_gla, wgate, wbr, wout)


FFN_COLS = 256


def _gelu_tanh(x):
    c = 0.7978845608028654
    return 0.5 * x * (1.0 + jnp.tanh(c * (x + 0.044715 * (x * x * x))))


def _ffn_kernel(*refs, nseg, seg, dff, has_prev, has_final):
    refs = list(refs)
    x_ref, g_ref = refs[:2]
    prev_ref = refs[2] if has_prev else None
    wup_ref, cw_ref, cb_ref, wdn_ref = refs[2 + has_prev:6 + has_prev]
    fg_ref = refs[6 + has_prev] if has_final else None
    o_ref, st_ref, ext_ref, gact_ref, tail_ref = refs[6 + has_prev + has_final:]
    t = pl.program_id(1)
    xf = x_ref[...]
    h = _rmsnorm(xf, g_ref[...]).astype(BF16)
    pad0 = CONV_PAD - (CONV_W - 1)

    if not has_prev:
        @pl.when(t == 0)
        def _():
            tail_ref[...] = jnp.zeros(tail_ref.shape, F32)

    def conv_half(c0):
        cols = slice(c0, c0 + FFN_COLS)
        up = _dot(h, wup_ref[:, cols])
        outs = []
        for s in range(nseg):
            if has_prev:
                ext_ref[s, pad0:CONV_PAD, :] = prev_ref[s, :, cols]
            else:
                ext_ref[s, 0:CONV_PAD, :] = tail_ref[:, cols]
            ext_ref[s, CONV_PAD:CONV_PAD + seg, :] = up[s * seg:(s + 1) * seg, :]
            acc = cb_ref[:, cols] + ext_ref[s, pad0:pad0 + seg, :] * cw_ref[0:1, cols]
            for i in range(1, CONV_W):
                acc = acc + ext_ref[s, pad0 + i:pad0 + i + seg, :] * cw_ref[i:i + 1, cols]
            outs.append(acc)
            if has_prev:
                st_ref[s, :, cols] = ext_ref[s, CONV_PAD + seg - (CONV_W - 1):CONV_PAD + seg, :]
            else:
                tail_ref[:, cols] = ext_ref[s, seg:seg + CONV_PAD, :]
        return outs[0] if nseg == 1 else jnp.concatenate(outs, axis=0)

    for c0 in range(0, dff, FFN_COLS):
        a = conv_half(c0)
        b = conv_half(dff + c0)
        gact_ref[:, c0:c0 + FFN_COLS] = (_gelu_tanh(a) * b).astype(BF16)

    y = xf + _dot(gact_ref[...], wdn_ref[...])
    o_ref[...] = _rmsnorm(y, fg_ref[...]) if has_final else y

    if not has_prev:
        @pl.when(t == pl.num_programs(1) - 1)
        def _():
            st_ref[0] = tail_ref[CONV_PAD - (CONV_W - 1):CONV_PAD, :]


def _ffn(x, g2, prev, wup, cw, cb, wdn, final_g, *, batch, seq):
    n, d = x.shape
    dff = wdn.shape[0]
    has_prev = prev is not None
    has_final = final_g is not None
    if has_prev:
        nseg = max(1, min(ROW_TILE // seq, batch))
        seg = seq
        grid = (batch // nseg, 1)
        xmap = lambda i, t: (i, 0)
        stmap = lambda i, t: (i, 0, 0)
    else:
        nseg = 1
        seg = min(ROW_TILE, seq)
        grid = (batch, seq // seg)
        nt = seq // seg
        xmap = lambda b, t: (b * nt + t, 0)
        stmap = lambda b, t: (b, 0, 0)
    tm = nseg * seg
    in_specs = [pl.BlockSpec((tm, d), xmap), _const_spec(g2.shape)]
    args = [x, g2]
    if has_prev:
        in_specs.append(pl.BlockSpec((nseg, CONV_W - 1, 2 * dff), stmap))
        args.append(prev)
    in_specs += [_const_spec(wup.shape), _const_spec(cw.shape), _const_spec(cb.shape), _const_spec(wdn.shape)]
    args += [wup, cw, cb, wdn]
    if has_final:
        in_specs.append(_const_spec(final_g.shape))
        args.append(final_g)
    vmem = (2 * (2 * _nbytes((tm, d), F32) + 2 * _nbytes((nseg, CONV_W - 1, 2 * dff), F32))
            + _nbytes(wup.shape, BF16) + _nbytes(wdn.shape, BF16) + 2 * _nbytes(cw.shape, F32)
            + _nbytes((nseg, CONV_PAD + seg, FFN_COLS), F32) + _nbytes((tm, dff), BF16)
            + _nbytes((CONV_PAD, 2 * dff), F32) + 4 * _nbytes((tm, d), F32) + 12 * _nbytes((tm, FFN_COLS), F32))
    o, st = pl.pallas_call(
        functools.partial(_ffn_kernel, nseg=nseg, seg=seg, dff=dff, has_prev=has_prev, has_final=has_final),
        out_shape=(jax.ShapeDtypeStruct((n, d), F32),
                   jax.ShapeDtypeStruct((batch, CONV_W - 1, 2 * dff), F32)),
        grid=grid,
        in_specs=in_specs,
        out_specs=(pl.BlockSpec((tm, d), xmap), pl.BlockSpec((nseg, CONV_W - 1, 2 * dff), stmap)),
        scratch_shapes=[pltpu.VMEM((nseg, CONV_PAD + seg, FFN_COLS), F32),
                        pltpu.VMEM((tm, dff), BF16),
                        pltpu.VMEM((CONV_PAD, 2 * dff), F32)],
        compiler_params=_cparams(2, vmem),
        name="ffn",
    )(*args)
    return o, st


def _layer(x, w, kstack, vstack, *, layer, depth, batch, seq, start, cache_k, cache_v, prev_pool, prev_gla,
           prev_conv):
    d = x.shape[-1]
    bw, kw = d // 2, d // 4
    sb, kstack, vstack, u, gf, vg = _inproj(x, w["g1"], w["wa"], w["wl"], w["wg"], w["bg"], kstack, vstack,
                                            layer=layer, depth=depth, bw=bw, kw=kw)
    if cache_k is None:
        o_sb = _sb_prompt(sb, batch=batch, seq=seq, bw=bw)
    else:
        o_sb = _sb_sample(sb, cache_k, cache_v, layer=layer, batch=batch, seq=seq, bw=bw)
    o_pool, pool_state = _pool(u, prev_pool, w["w_pool"], w["pool_scale"], batch=batch, seq=seq, start=start)
    o_gla, gla_state = _gla(gf, vg, prev_gla, w["gnorm"], batch=batch, seq=seq, kw=kw, vw=bw)
    x = _merge(x, w["g1"], o_sb, o_pool, o_gla, w["wgate"], w["wbr"], w["wout"])
    x, conv_state = _ffn(x, w["g2"], prev_conv, w["wup"], w["cw"], w["cb"], w["wdn"],
                         w["final_g"] if layer == depth - 1 else None, batch=batch, seq=seq)
    return x, kstack, vstack, (pool_state, gla_state, conv_state)


def kernel(x_prompt, x_sample, cache_k, cache_v, state_pool, state_gla, state_conv, norm1_g, w_in, w_gla_gate,
           b_gla_gate, gla_norm_g, w_pool, pool_scale, w_branch, w_out, norm2_g, w_up, conv_w, conv_b, w_down,
           final_g):
    bp, tp, d = x_prompt.shape
    bs, ts, _ = x_sample.shape
    depth = w_in.shape[0]
    past = cache_k.shape[2]
    bw, kw = d // 2, d // 4
    rank = w_gla_gate.shape[1]
    n_main = 4 * bw + 2 * kw + 2 * bw
    xp = x_prompt.reshape(bp * tp, d)
    xs = x_sample.reshape(bs * ts, d)
    outs_p, outs_s = [], []
    kp = vp = ks = vs = None
    cache_kt = jnp.transpose(cache_k, (0, 1, 3, 4, 2)).reshape(depth, bs, bw, past)
    cache_vt = jnp.transpose(cache_v, (0, 1, 3, 4, 2)).reshape(depth, bs, bw, past)
    for l in range(depth):
        wl_pad = jnp.zeros((d, LANES), F32).at[:, :rank].set(w_in[l, :, n_main:n_main + rank])
        wg_pad = jnp.zeros((LANES, kw), F32).at[:rank, :].set(w_gla_gate[l])
        w = dict(
            g1=norm1_g[l].reshape(1, d), g2=norm2_g[l].reshape(1, d),
            wa=w_in[l, :, :n_main].astype(BF16), wl=wl_pad.astype(BF16), wg=wg_pad.astype(BF16),
            bg=b_gla_gate[l].reshape(1, kw),
            wgate=w_in[l, :, n_main + rank:].astype(BF16),
            w_pool=w_pool[l].astype(BF16), pool_scale=pool_scale[l].reshape(1, bw),
            gnorm=gla_norm_g[l].reshape(1, -1),
            wbr=w_branch[l].astype(BF16), wout=w_out[l].astype(BF16),
            wup=w_up[l].astype(BF16), cw=conv_w[l], cb=conv_b[l].reshape(1, -1), wdn=w_down[l].astype(BF16),
            final_g=final_g.reshape(1, d),
        )
        xp, kp, vp, st = _layer(xp, w, kp, vp, layer=l, depth=depth, batch=bp, seq=tp, start=0,
                                cache_k=None, cache_v=None, prev_pool=None, prev_gla=None, prev_conv=None)
        outs_p.append(st)
        xs, ks, vs, st = _layer(xs, w, ks, vs, layer=l, depth=depth, batch=bs, seq=ts, start=past,
                                cache_k=cache_kt, cache_v=cache_vt, prev_pool=state_pool[l], prev_gla=state_gla[l],
                                prev_conv=state_conv[l])
        outs_s.append(st)
    y_prompt = xp.reshape(bp, tp, d)
    y_sample = xs.reshape(bs, ts, d)
    stack =lambda outs, i: jnp.stack([o[i] for o in outs])
    heads = bw // SB_HEAD_DIM
    kv_p = (depth, bp, tp, heads, SB_HEAD_DIM)
    kv_s = (depth, bs, ts, heads, SB_HEAD_DIM)
    return (y_prompt, y_sample,
            kp.reshape(kv_p), vp.reshape(kv_p), stack(outs_p, 0), stack(outs_p, 1), stack(outs_p, 2),
            ks.reshape(kv_s), vs.reshape(kv_s), stack(outs_s, 0), stack(outs_s, 1), stack(outs_s, 2))
```

```python
import functools

import jax
import jax.numpy as jnp
from jax import lax
from jax.experimental import pallas as pl
from jax.experimental.pallas import tpu as pltpu

F32 = jnp.float32
BF16 = jnp.bfloat16

NORM_EPS = 1e-6
LOG2E = 1.4426950408889634
CHUNK = 64
SB_HEAD_DIM = 64
POOL_WINDOWS = (2, 4, 8, 16)
POOL_STATE = max(POOL_WINDOWS) - 1
POOL_PAD = 32
GLA_HEADS = 4
GLA_GATE_TAU = 16.0
CONV_W = 3
CONV_PAD = 8
LANES = 128
V7X_SCOPED_VMEM_BYTES = 60000 * 1024

ROW_TILE = 512
SB_BLOCK = 256
SB_PAIRS = 4
SB_DEAD_LOG = -150.0


def _cparams(n_axes, vmem_bytes):
    return pltpu.CompilerParams(
        dimension_semantics=("arbitrary",) * n_axes,
        vmem_limit_bytes=int(min(vmem_bytes, V7X_SCOPED_VMEM_BYTES)),
    )


def _nbytes(shape, dtype):
    n = 1
    for s in shape:
        n *= s
    return n * jnp.dtype(dtype).itemsize


def _const_spec(shape):
    nd = len(shape)
    return pl.BlockSpec(shape, lambda *_: (0,) * nd, pipeline_mode=pl.Buffered(1))


def _rmsnorm(xf, g):
    ms = jnp.mean(xf * xf, axis=-1, keepdims=True)
    return (xf * lax.rsqrt(ms + NORM_EPS)) * g


def _log_sigmoid(x):
    return jnp.minimum(x, 0.0) - jnp.log1p(jnp.exp(-jnp.abs(x)))


def _split_bf16(x):
    hi = x.astype(BF16)
    lo = (x - hi.astype(F32)).astype(BF16)
    return hi, lo


def _dot(a, b):
    return jnp.dot(a, b, preferred_element_type=F32)


def _dot_nt(a, b):
    return lax.dot_general(a, b, (((1,), (1,)), ((), ())), preferred_element_type=F32)


def _dot_tn(a, b):
    return lax.dot_general(a, b, (((0,), (0,)), ((), ())), preferred_element_type=F32)


def _inproj_kernel(*refs, bw, kw, aliased):
    x_ref, g_ref, wa_ref, wl_ref, wg_ref, bg_ref = refs[:6]
    sb_ref, k_ref, v_ref, u_ref, gf_ref, vg_ref, h_ref = refs[6 + 2 * aliased:]
    h = _rmsnorm(x_ref[...], g_ref[...]).astype(BF16)
    h_ref[...] = h
    tm = x_ref.shape[0]
    heads = bw // SB_HEAD_DIM

    def proj(c0, c1):
        return _dot(h, wa_ref[:, c0:c1])

    def store_heads(dst_ref, val):
        for hd in range(heads):
            dst_ref[pl.ds(hd, tm, stride=heads), :] = val[:, hd * SB_HEAD_DIM:(hd + 1) * SB_HEAD_DIM]

    sb_ref[:, 0:bw] = (proj(0, bw) * (SB_HEAD_DIM ** -0.5 * LOG2E)).astype(BF16)
    k = proj(bw, 2 * bw)
    store_heads(k_ref, k)
    sb_ref[:, bw:2 * bw] = k.astype(BF16)
    v = proj(2 * bw, 3 * bw)
    store_heads(v_ref, v)
    sb_ref[:, 2 * bw:3 * bw] = v.astype(BF16)
    u_ref[...] = proj(3 * bw, 4 * bw)
    o = 4 * bw
    gf_ref[:, 0:kw] = proj(o, o + kw)
    gf_ref[:, kw:2 * kw] = proj(o + kw, o + 2 * kw)
    vg_ref[...] = proj(o + 2 * kw, o + 2 * kw + bw).astype(BF16)
    gf_ref[:, 3 * kw:3 * kw + bw] = proj(o + 2 * kw + bw, o + 2 * kw + 2 * bw)
    g_lr = _dot(h, wl_ref[...]).astype(BF16)
    pre = _dot(g_lr, wg_ref[...]) + bg_ref[...]
    gf_ref[:, 2 * kw:3 * kw] = _log_sigmoid(pre) * (1.0 / GLA_GATE_TAU)


def _inproj(x, g1, wa, wl, wg, bg, kstack, vstack, *, layer, depth, bw, kw):
    n, d = x.shape
    tm = min(ROW_TILE, n)
    gfw = 3 * kw + bw
    heads = bw // SB_HEAD_DIM
    aliased = kstack is not None
    kv_shape = jax.ShapeDtypeStruct((depth, n * heads, SB_HEAD_DIM), F32)
    out_shape = (
        jax.ShapeDtypeStruct((n, 3 * bw), BF16),
        kv_shape,
        kv_shape,
        jax.ShapeDtypeStruct((n, bw), F32),
        jax.ShapeDtypeStruct((n, gfw), F32),
        jax.ShapeDtypeStruct((n, bw), BF16),
        jax.ShapeDtypeStruct((n, d), BF16),
    )
    row = lambda w: pl.BlockSpec((tm, w), lambda i: (i, 0))
    slab = pl.BlockSpec((None, tm * heads, SB_HEAD_DIM), lambda i: (layer, i, 0))
    blocks = 2 * (_nbytes((tm, d), F32) + _nbytes((tm, 3 * bw), BF16) + 2 * _nbytes((tm, heads, LANES), F32)
                  + _nbytes((tm, bw), F32) + _nbytes((tm, gfw), F32) + _nbytes((tm, bw), BF16)
                  + _nbytes((tm, d), BF16))
    weights = _nbytes(wa.shape, BF16) + _nbytes(wl.shape, BF16) + _nbytes(wg.shape, BF16)
    temps = 4 * _nbytes((tm, d), F32)
    in_specs = [row(d), _const_spec(g1.shape), _const_spec(wa.shape), _const_spec(wl.shape),
                _const_spec(wg.shape), _const_spec(bg.shape)]
    args = [x, g1, wa, wl, wg, bg]
    if aliased:
        in_specs += [pl.BlockSpec(memory_space=pl.ANY)] * 2
        args += [kstack, vstack]
    return pl.pallas_call(
        functools.partial(_inproj_kernel, bw=bw, kw=kw, aliased=aliased),
        out_shape=out_shape,
        grid=(n // tm,),
        in_specs=in_specs,
        out_specs=(row(3 * bw), slab, slab, row(bw), row(gfw), row(bw), row(d)),
        input_output_aliases={6: 1, 7: 2} if aliased else {},
        compiler_params=_cparams(1, blocks + weights + temps),
        name="inproj",
    )(*args)


def _suffix_sum_matrix(n):
    r = lax.broadcasted_iota(jnp.int32, (n, n), 0)
    c = lax.broadcasted_iota(jnp.int32, (n, n), 1)
    return jnp.where(r > c, 1.0, 0.0).astype(BF16)


def _sb_chains(qs, kbs, vbs, u, carries, masks, keys_on_lanes=False):
    n = len(qs)
    sign = jnp.uint32(0x80000000)
    ls_pos, ls_neg, ws, new, pvs = {}, {}, {}, [], []
    for t in range(n + 2):
        if t < n:
            s = _dot(qs[t], kbs[t]) if keys_on_lanes else _dot_nt(qs[t], kbs[t])
            nabs = pltpu.bitcast(pltpu.bitcast(s, jnp.uint32) | sign, F32)
            sp = jnp.log(1.0 + jnp.exp2(nabs)) * LOG2E
            pos = jnp.minimum(s, 0.0) - sp
            neg = pos - s
            ls_pos[t] = pos
            ls_neg[t] = neg if masks[t] is None else jnp.where(masks[t], neg, 0.0)
        i = t - 1
        if 0 <= i < n:
            carry = new[carries[i]] if isinstance(carries[i], int) else carries[i]
            later = _dot(ls_neg[i].astype(BF16), u)
            w = jnp.exp2(ls_pos.pop(i) + later + carry)
            ws[i] = (w if masks[i] is None else jnp.where(masks[i], w, 0.0)).astype(BF16)
            new.append(carry + jnp.sum(ls_neg.pop(i), axis=-1, keepdims=True))
        i = t - 2
        if 0 <= i < n:
            pvs.append(_dot_nt(ws.pop(i), vbs[i]) if keys_on_lanes else _dot(ws.pop(i), vbs[i]))
    return pvs, new


def _head_masks():
    lane = lax.broadcasted_iota(jnp.int32, (1, LANES), 1)
    return lane < SB_HEAD_DIM


def _max_all(carries):
    m = carries[0]
    for c in carries[1:]:
        m = jnp.maximum(m, c)
    return jnp.max(m)


def _sb_prompt_kernel(q_ref, k_ref, v_ref, o_ref, acc_ref, carry_ref, *, blk):
    i = pl.program_id(2)
    npair = q_ref.shape[-1] // LANES
    heads = 2 * npair
    first = _head_masks()
    u = _suffix_sum_matrix(blk)
    r = lax.broadcasted_iota(jnp.int32, (blk, blk), 0)
    c = lax.broadcasted_iota(jnp.int32, (blk, blk), 1)
    causal = c < r
    qs = []
    for p in range(npair):
        q = q_ref[0, :, p * LANES:(p + 1) * LANES]
        qs += [jnp.where(first, q, jnp.zeros_like(q)), jnp.where(first, jnp.zeros_like(q), q)]

    def operands(ref, j):
        off = pl.multiple_of(j * blk, blk)
        blocks = [ref[0, pl.ds(off, blk), p * LANES:(p + 1) * LANES] for p in range(npair)]
        return [blocks[hd // 2] for hd in range(heads)]

    def sweep(j):
        pvs, new = _sb_chains(qs, operands(k_ref, j), operands(v_ref, j), u,
                              [carry_ref[hd] for hd in range(heads)], [None] * heads)
        for hd in range(heads):
            carry_ref[hd] = new[hd]
            acc_ref[hd] += pvs[hd]
        return _max_all(new)

    zero = jnp.zeros((blk, 1), F32)

    @pl.when(i == 0)
    def _():
        pvs, _ = _sb_chains(qs, operands(k_ref, 0), operands(v_ref, 0), u, [zero] * heads, [causal] * heads)
        for hd in range(heads):
            acc_ref[hd] = pvs[hd]

    @pl.when(i > 0)
    def _():
        pvs, new = _sb_chains(qs * 2, operands(k_ref, i) + operands(k_ref, i - 1),
                              operands(v_ref, i) + operands(v_ref, i - 1), u,
                              [zero] * heads + list(range(heads)), [causal] * heads + [None] * heads)
        for hd in range(heads):
            acc_ref[hd] = pvs[hd] + pvs[heads + hd]
            carry_ref[hd] = new[heads + hd]

        def cond(state):
            return jnp.logical_and(state[0] >= 0, state[1] >= SB_DEAD_LOG)

        def body(state):
            return (state[0] - 1, sweep(state[0]))

        lax.while_loop(cond, body, (i - 2, _max_all(new[heads:])))

    for p in range(npair):
        o_ref[0, :, p * LANES:(p + 1) * LANES] = jnp.where(first, acc_ref[2 * p], acc_ref[2 * p + 1]).astype(o_ref.dtype)


def _sb_prompt(sb, *, batch, seq, bw):
    blk = min(SB_BLOCK, seq)
    nq = seq // blk
    width = SB_PAIRS * LANES
    ngrp = bw // width
    sb3 = sb.reshape(batch, seq, 3 * bw)
    vmem = (2 * (2 * _nbytes((blk, width), BF16) + 2 * _nbytes((seq, width), BF16))
            + 4 * SB_PAIRS * _nbytes((blk, LANES), F32) + 2 * SB_PAIRS * 8 * _nbytes((blk, blk), F32))
    out = pl.pallas_call(
        functools.partial(_sb_prompt_kernel, blk=blk),
        out_shape=jax.ShapeDtypeStruct((batch, seq, bw), BF16),
        grid=(batch, ngrp, nq),
        in_specs=[
            pl.BlockSpec((1, blk, width), lambda b, p, i: (b, i, p)),
            pl.BlockSpec((1, seq, width), lambda b, p, i: (b, 0, ngrp + p)),
            pl.BlockSpec((1, seq, width), lambda b, p, i: (b, 0, 2 * ngrp + p)),
        ],
        out_specs=pl.BlockSpec((1, blk, width), lambda b, p, i: (b, i, p)),
        scratch_shapes=[pltpu.VMEM((2 * SB_PAIRS, blk, LANES), F32), pltpu.VMEM((2 * SB_PAIRS, blk, 1), F32)],
        compiler_params=_cparams(3, vmem),
        name="sb_prompt",
    )(sb3, sb3, sb3)
    return out.reshape(batch * seq, bw)


def _sb_sample_kernel(q_ref, kn_ref, vn_ref, kp_hbm, vp_hbm, o_ref, acc_ref, carry_ref, kbuf, vbuf, sem, *,
                      layer, tq, blk, npast):
    b = pl.program_id(0)
    heads = acc_ref.shape[0]
    npair = heads // 2
    first = _head_masks()
    u_new = _suffix_sum_matrix(tq)
    u = _suffix_sum_matrix(blk)
    r = lax.broadcasted_iota(jnp.int32, (tq, tq), 0)
    c = lax.broadcasted_iota(jnp.int32, (tq, tq), 1)
    causal = c < r

    def block_copies(j):
        rows = pl.ds(pl.multiple_of(j * blk, blk), blk)
        return (pltpu.make_async_copy(kp_hbm.at[layer, b, :, rows], kbuf, sem.at[0]),
                pltpu.make_async_copy(vp_hbm.at[layer, b, :, rows], vbuf, sem.at[1]))

    for cp in block_copies(npast - 1):
        cp.start()

    qs = []
    for p in range(npair):
        q = q_ref[0, :, p * LANES:(p + 1) * LANES]
        qs += [jnp.where(first, q, jnp.zeros_like(q)), jnp.where(first, jnp.zeros_like(q), q)]

    pair_lanes = [slice((hd // 2) * LANES, (hd // 2 + 1) * LANES) for hd in range(heads)]
    pvs, carries = _sb_chains(qs, [kn_ref[0, :, ln] for ln in pair_lanes], [vn_ref[0, :, ln] for ln in pair_lanes],
                              u_new, [jnp.zeros((tq, 1), F32)] * heads, [causal] * heads)
    for hd in range(heads):
        acc_ref[hd] = pvs[hd]
        carry_ref[hd] = carries[hd]

    def cond(state):
        return state[1] > 0

    def body(state):
        j = state[0]
        for cp in block_copies(j):
            cp.wait()
        kbs = [kbuf[p * LANES:(p + 1) * LANES, :].astype(BF16) for p in range(npair)]
        vbs = [vbuf[p * LANES:(p + 1) * LANES, :].astype(BF16) for p in range(npair)]
        pvs, new = _sb_chains(qs, [kbs[hd // 2] for hd in range(heads)], [vbs[hd // 2] for hd in range(heads)],
                              u, [carry_ref[hd] for hd in range(heads)], [None] * heads, keys_on_lanes=True)
        for hd in range(heads):
            acc_ref[hd] += pvs[hd]
            carry_ref[hd] = new[hd]
        more = jnp.logical_and(j > 0, _max_all(new) >= SB_DEAD_LOG)

        @pl.when(more)
        def _():
            for cp in block_copies(j - 1):
                cp.start()

        return (j - 1, more.astype(jnp.int32))

    alive = _max_all(carries) >= SB_DEAD_LOG

    @pl.when(jnp.logical_not(alive))
    def _():
        for cp in block_copies(npast - 1):
            cp.wait()

    lax.while_loop(cond, body, (jnp.int32(npast - 1), alive.astype(jnp.int32)))
    for p in range(npair):
        o_ref[0, :, p * LANES:(p + 1) * LANES] = jnp.where(first, acc_ref[2 * p], acc_ref[2 * p + 1]).astype(o_ref.dtype)


def _sb_sample(sb, cache_k, cache_v, *, layer, batch, seq, bw):
    depth, _, _, past = cache_k.shape
    heads = bw // SB_HEAD_DIM
    blk = min(SB_BLOCK, past)
    sb3 = sb.reshape(batch, seq, 3 * bw)
    vmem = (2 * 4 * _nbytes((seq, bw), BF16) + 2 * _nbytes((bw, blk), F32)
            + 2 * heads * _nbytes((seq, LANES), F32) + heads * 8 * _nbytes((seq, blk), F32))
    new = lambda col: pl.BlockSpec((1, seq, bw), lambda b: (b, 0, col))
    out = pl.pallas_call(
        functools.partial(_sb_sample_kernel, layer=layer, tq=seq, blk=blk, npast=past // blk),
        out_shape=jax.ShapeDtypeStruct((batch, seq, bw), BF16),
        grid=(batch,),
        in_specs=[new(0), new(1), new(2), pl.BlockSpec(memory_space=pl.ANY), pl.BlockSpec(memory_space=pl.ANY)],
        out_specs=new(0),
        scratch_shapes=[pltpu.VMEM((heads, seq, LANES), F32),
                        pltpu.VMEM((heads, seq, 1), F32),
                        pltpu.VMEM((bw, blk), F32),
                        pltpu.VMEM((bw, blk), F32),
                        pltpu.SemaphoreType.DMA((2,))],
        compiler_params=_cparams(1, vmem),
        name="sb_sample",
    )(sb3, sb3, sb3, cache_k, cache_v)
    return out.reshape(batch * seq, bw)


def _pool_kernel(*refs, tq, gdim, start, has_prev):
    if has_prev:
        u_ref, prev_ref, w_ref, sc_ref, o_ref, st_ref, ext_ref, lvl_ref = refs
    else:
        u_ref, w_ref, sc_ref, o_ref, st_ref, ext_ref, lvl_ref = refs
    t = pl.program_id(1)

    @pl.when(t == 0)
    def _():
        ext_ref[0:POOL_PAD, :] = jnp.zeros((POOL_PAD, ext_ref.shape[1]), F32)
        if has_prev:
            ext_ref[POOL_PAD - POOL_STATE:POOL_PAD, :] = prev_ref[0]

    @pl.when(t > 0)
    def _():
        ext_ref[0:POOL_PAD, :] = ext_ref[tq:tq + POOL_PAD, :]

    ext_ref[POOL_PAD:POOL_PAD + tq, :] = u_ref[0]
    pos = start + t * tq + lax.broadcasted_iota(jnp.int32, (tq, 1), 0)
    nrow = POOL_PAD + tq
    for g, w in enumerate(POOL_WINDOWS):
        cols = slice(g * gdim, (g + 1) * gdim)
        tok = ext_ref[POOL_PAD:nrow, cols]
        def level(slot, r0, r1):
            return ext_ref[r0:r1, cols] if slot is None else lvl_ref[slot, r0:r1, :]

        slot, lo, m = None, 0, 1
        while 2 * m < w:
            lo += 8
            nxt = 0 if slot != 0 else 1
            lvl_ref[nxt, lo:nrow, :] = level(slot, lo, nrow) + level(slot, lo - m, nrow - m)
            slot, m = nxt, 2 * m
        s = level(slot, POOL_PAD, nrow) + level(slot, POOL_PAD - m, nrow - m)
        cnt = jnp.minimum(pos + 1, w).astype(F32)
        pooled = (s / cnt - tok).astype(BF16)
        o_ref[0, :, cols] = (_dot(pooled, w_ref[g]) * sc_ref[:, cols]).astype(o_ref.dtype)

    @pl.when(t == pl.num_programs(1) - 1)
    def _():
        st_ref[0] = ext_ref[POOL_PAD + tq - POOL_STATE:POOL_PAD + tq, :]


def _pool(u, prev, w_pool, scale, *, batch, seq, start):
    bw = u.shape[-1]
    tq = min(ROW_TILE, seq)
    gdim = bw // len(POOL_WINDOWS)
    has_prev = prev is not None
    u3 = u.reshape(batch, seq, bw)
    in_specs = [pl.BlockSpec((1, tq, bw), lambda b, t: (b, t, 0))]
    args = [u3]
    if has_prev:
        in_specs.append(pl.BlockSpec((1, POOL_STATE, bw), lambda b, t: (b, 0, 0)))
        args.append(prev)
    in_specs += [_const_spec(w_pool.shape), _const_spec(scale.shape)]
    args += [w_pool, scale]
    vmem = (2 * (_nbytes((tq, bw), F32) + _nbytes((tq, bw), BF16) + 2 * _nbytes((POOL_PAD, bw), F32))
            + _nbytes((POOL_PAD + tq, bw), F32) + 2 * _nbytes((POOL_PAD + tq, gdim), F32)
            + _nbytes(w_pool.shape, BF16) + 6 * _nbytes((tq, bw), F32))
    o, st = pl.pallas_call(
        functools.partial(_pool_kernel, tq=tq, gdim=gdim, start=start, has_prev=has_prev),
        out_shape=(jax.ShapeDtypeStruct((batch, seq, bw), BF16),
                   jax.ShapeDtypeStruct((batch, POOL_STATE, bw), F32)),
        grid=(batch, seq // tq),
        in_specs=in_specs,
        out_specs=(pl.BlockSpec((1, tq, bw), lambda b, t: (b, t, 0)),
                   pl.BlockSpec((1, POOL_STATE, bw), lambda b, t: (b, 0, 0))),
        scratch_shapes=[pltpu.VMEM((POOL_PAD + tq, bw), F32), pltpu.VMEM((2, POOL_PAD + tq, gdim), F32)],
        compiler_params=_cparams(2, vmem),
        name="pool",
    )(*args)
    return o.reshape(batch * seq, bw), st


def _gla_kernel(*refs, tt, kw, vw, has_state):
    if has_state:
        gf_ref, vg_ref, s0_ref, gn_ref, o_ref, s_out_ref, s_ref, og_ref = refs
    else:
        gf_ref, vg_ref, gn_ref, o_ref, s_out_ref, s_ref, og_ref = refs
    t = pl.program_id(1)
    dk = kw // GLA_HEADS
    dv = vw // GLA_HEADS
    nh = GLA_HEADS

    @pl.when(t == 0)
    def _():
        if has_state:
            s_ref[...] = s0_ref[0]
        else:
            s_ref[...] = jnp.zeros(s_ref.shape, F32)

    r = lax.broadcasted_iota(jnp.int32, (CHUNK, CHUNK), 0)
    c = lax.broadcasted_iota(jnp.int32, (CHUNK, CHUNK), 1)
    tril = jnp.where(r >= c, 1.0, 0.0).astype(BF16)
    lane_head = lax.broadcasted_iota(jnp.int32, (CHUNK, kw), 1) // dk
    head_lanes = [lane_head == h for h in range(nh)]
    qi = lax.broadcasted_iota(jnp.int32, (nh * CHUNK, CHUNK), 0) % CHUNK
    kj = lax.broadcasted_iota(jnp.int32, (nh * CHUNK, CHUNK), 1)
    causal = qi >= kj

    def stack_heads(x):
        return jnp.concatenate([jnp.where(m, x, 0.0) for m in head_lanes], axis=0).astype(BF16)

    def diag(x, h, width):
        rows = x.shape[0] // nh
        return x[h * rows:(h + 1) * rows, h * width:(h + 1) * width]

    def prepare(ci):
        rows = slice(ci * CHUNK, (ci + 1) * CHUNK)
        q = gf_ref[0, rows, 0:kw] * (dk ** -0.5)
        k = gf_ref[0, rows, kw:2 * kw]
        hi, lo = _split_bf16(gf_ref[0, rows, 2 * kw:3 * kw])
        b = _dot(tril, hi) + _dot(tril, lo)
        bref = b[CHUNK // 2:CHUNK // 2 + 1, :]
        blast = b[CHUNK - 1:CHUNK, :]
        blast_col = jnp.broadcast_to(blast, (dv, kw)).T
        return dict(qe=stack_heads(q * jnp.exp(b - bref)), qb=stack_heads(q * jnp.exp(b)),
                    ke=(k * jnp.exp(bref - b)).astype(BF16), kl=(k * jnp.exp(blast - b)).astype(BF16),
                    decay=jnp.exp(blast_col))

    def scores(ci, p):
        v = vg_ref[0, ci * CHUNK:(ci + 1) * CHUNK, :]
        att = jnp.where(causal, _dot_nt(p.pop("qe"), p.pop("ke")), 0.0).astype(BF16)
        upd = _dot_tn(p.pop("kl"), v)
        p.update(att=att, upd=jnp.concatenate([diag(upd, h, dv) for h in range(nh)], axis=0))

    def finish(ci, p):
        rows = slice(ci * CHUNK, (ci + 1) * CHUNK)
        s_old = s_ref[...]
        inter = _dot(p["qb"], s_old.astype(BF16))
        for h in range(nh):
            hrows = slice(h * CHUNK, (h + 1) * CHUNK)
            intra = _dot(p["att"][hrows, :], vg_ref[0, rows, h * dv:(h + 1) * dv])
            og_ref[rows, h * dv:(h + 1) * dv] = intra + inter[hrows, :]
        s_ref[...] = s_old * p["decay"] + p["upd"]

    nchunk = tt // CHUNK
    work = {}
    for step in range(nchunk + 2):
        if step < nchunk:
            work[step] = prepare(step)
        if 0 <= step - 1 < nchunk:
            scores(step - 1, work[step - 1])
        if 0 <= step - 2 < nchunk:
            finish(step - 2, work.pop(step - 2))

    gn = gn_ref[...]
    for h in range(GLA_HEADS):
        cols = slice(h * dv, (h + 1) * dv)
        y = _rmsnorm(og_ref[:, cols], gn)
        rg = gf_ref[0, :, 3 * kw + h * dv:3 * kw + (h + 1) * dv]
        o_ref[0, :, cols] = (y * (rg / (1.0 + jnp.exp(-rg)))).astype(o_ref.dtype)

    @pl.when(t == pl.num_programs(1) - 1)
    def _():
        s_out_ref[0] = s_ref[...]


def _gla(gf, vg, s0, gnorm, *, batch, seq, kw, vw):
    tt = min(ROW_TILE, seq)
    dk, dv = kw // GLA_HEADS, vw // GLA_HEADS
    gfw = gf.shape[-1]
    has_state = s0 is not None
    in_specs = [pl.BlockSpec((1, tt, gfw), lambda b, t: (b, t, 0)),
                pl.BlockSpec((1, tt, vw), lambda b, t: (b, t, 0))]
    args = [gf.reshape(batch, seq, gfw), vg.reshape(batch, seq, vw)]
    state_spec = pl.BlockSpec((1, kw, dv), lambda b, t: (b, 0, 0))
    if has_state:
        in_specs.append(state_spec)
        args.append(s0.reshape(batch, kw, dv))
    in_specs.append(_const_spec(gnorm.shape))
    args.append(gnorm)
    vmem = (2 * (_nbytes((tt, gfw), F32) + 2 * _nbytes((tt, vw), BF16) + 2 * _nbytes((kw, dv), F32))
            + _nbytes((kw, dv), F32) + _nbytes((tt, vw), F32) + 16 * _nbytes((GLA_HEADS * CHUNK, vw), F32))
    o, st = pl.pallas_call(
        functools.partial(_gla_kernel, tt=tt, kw=kw, vw=vw, has_state=has_state),
        out_shape=(jax.ShapeDtypeStruct((batch, seq, vw), BF16),
                   jax.ShapeDtypeStruct((batch, kw, dv), F32)),
        grid=(batch, seq // tt),
        in_specs=in_specs,
        out_specs=(pl.BlockSpec((1, tt, vw), lambda b, t: (b, t, 0)), state_spec),
        scratch_shapes=[pltpu.VMEM((kw, dv), F32), pltpu.VMEM((tt, vw), F32)],
        compiler_params=_cparams(2, vmem),
        name="gla",
    )(*args)
    return o.reshape(batch * seq, vw), st.reshape(batch, GLA_HEADS, dk, dv)


def _merge_kernel(x_ref, h_ref, g2_ref, osb_ref, opool_ref, ogla_ref, wgate_ref, wbr_ref, wout_ref, o_ref, h2_ref,
                  *, d):
    h = h_ref[...]
    m = None
    for n, br_ref in enumerate((osb_ref, opool_ref, ogla_ref)):
        logits = _dot(h, wgate_ref[:, n * d:(n + 1) * d])
        gate = 1.0 / (1.0 + jnp.exp(-logits))
        term = gate * _dot(br_ref[...], wbr_ref[n])
        m = term if m is None else m + term
    y = x_ref[...] + _dot(m.astype(BF16), wout_ref[...])
    o_ref[...] = y
    h2_ref[...] = _rmsnorm(y, g2_ref[...]).astype(BF16)


def _merge(x, h1, g2, o_sb, o_pool, o_gla, wgate, wbr, wout):
    n, d = x.shape
    bw = o_sb.shape[-1]
    tm = min(ROW_TILE, n)
    row = lambda w: pl.BlockSpec((tm, w), lambda i: (i, 0))
    vmem = (2 * (2 * _nbytes((tm, d), F32) + 2 * _nbytes((tm, d), BF16) + 3 * _nbytes((tm, bw), BF16))
            + _nbytes(wgate.shape, BF16) + _nbytes(wbr.shape, BF16) + _nbytes(wout.shape, BF16)
            + 6 * _nbytes((tm, d), F32))
    return pl.pallas_call(
        functools.partial(_merge_kernel, d=d),
        out_shape=(jax.ShapeDtypeStruct((n, d), F32), jax.ShapeDtypeStruct((n, d), BF16)),
        grid=(n // tm,),
        in_specs=[row(d), row(d), _const_spec(g2.shape), row(bw), row(bw), row(bw),
                  _const_spec(wgate.shape), _const_spec(wbr.shape), _const_spec(wout.shape)],
        out_specs=(row(d), row(d)),
        compiler_params=_cparams(1, vmem),
        name="merge",
    )(x, h1, g2, o_sb, o_pool, o_gla, wgate, wbr, wout)


FFN_COLS = 256


def _gelu_tanh(x):
    c = 0.7978845608028654
    return 0.5 * x * (1.0 + jnp.tanh(c * (x + 0.044715 * (x * x * x))))


def _ffn_kernel(*refs, nseg, seg, dff, has_prev, has_final):
    refs = list(refs)
    x_ref, h_ref = refs[:2]
    prev_ref = refs[2] if has_prev else None
    wup_ref, cw_ref, cb_ref, wdn_ref = refs[2 + has_prev:6 + has_prev]
    fg_ref = refs[6 + has_prev] if has_final else None
    o_ref, st_ref, ext_ref, gact_ref, tail_ref = refs[6 + has_prev + has_final:]
    t = pl.program_id(1)
    h = h_ref[...]
    pad0 = CONV_PAD - (CONV_W - 1)

    if not has_prev:
        @pl.when(t == 0)
        def _():
            tail_ref[...] = jnp.zeros(tail_ref.shape, F32)

    def conv_half(c0):
        cols = slice(c0, c0 + FFN_COLS)
        up = _dot(h, wup_ref[:, cols])
        outs = []
        for s in range(nseg):
            if has_prev:
                ext_ref[s, pad0:CONV_PAD, :] = prev_ref[s, :, cols]
            else:
                ext_ref[s, 0:CONV_PAD, :] = tail_ref[:, cols]
            ext_ref[s, CONV_PAD:CONV_PAD + seg, :] = up[s * seg:(s + 1) * seg, :]
            acc = cb_ref[:, cols] + ext_ref[s, pad0:pad0 + seg, :] * cw_ref[0:1, cols]
            for i in range(1, CONV_W):
                acc = acc + ext_ref[s, pad0 + i:pad0 + i + seg, :] * cw_ref[i:i + 1, cols]
            outs.append(acc)
            if has_prev:
                st_ref[s, :, cols] = ext_ref[s, CONV_PAD + seg - (CONV_W - 1):CONV_PAD + seg, :]
            else:
                tail_ref[:, cols] = ext_ref[s, seg:seg + CONV_PAD, :]
        return outs[0] if nseg == 1 else jnp.concatenate(outs, axis=0)

    for c0 in range(0, dff, FFN_COLS):
        a = conv_half(c0)
        b = conv_half(dff + c0)
        gact_ref[:, c0:c0 + FFN_COLS] = (_gelu_tanh(a) * b).astype(BF16)

    y = x_ref[...] + _dot(gact_ref[...], wdn_ref[...])
    o_ref[...] = _rmsnorm(y, fg_ref[...]) if has_final else y

    if not has_prev:
        @pl.when(t == pl.num_programs(1) - 1)
        def _():
            st_ref[0] = tail_ref[CONV_PAD - (CONV_W - 1):CONV_PAD, :]


def _ffn(x, h2, prev, wup, cw, cb, wdn, final_g, *, batch, seq):
    n, d = x.shape
    dff = wdn.shape[0]
    has_prev = prev is not None
    has_final = final_g is not None
    if has_prev:
        nseg = max(1, min(ROW_TILE // seq, batch))
        seg = seq
        grid = (batch // nseg, 1)
        xmap = lambda i, t: (i, 0)
        stmap = lambda i, t: (i, 0, 0)
    else:
        nseg = 1
        seg = min(ROW_TILE, seq)
        grid = (batch, seq // seg)
        nt = seq // seg
        xmap = lambda b, t: (b * nt + t, 0)
        stmap = lambda b, t: (b, 0, 0)
    tm = nseg * seg
    in_specs = [pl.BlockSpec((tm, d), xmap), pl.BlockSpec((tm, d), xmap)]
    args = [x, h2]
    if has_prev:
        in_specs.append(pl.BlockSpec((nseg, CONV_W - 1, 2 * dff), stmap))
        args.append(prev)
    in_specs += [_const_spec(wup.shape), _const_spec(cw.shape), _const_spec(cb.shape), _const_spec(wdn.shape)]
    args += [wup, cw, cb, wdn]
    if has_final:
        in_specs.append(_const_spec(final_g.shape))
        args.append(final_g)
    vmem = (2 * (2 * _nbytes((tm, d), F32) + _nbytes((tm, d), BF16) + 2 * _nbytes((nseg, CONV_W - 1, 2 * dff), F32))
            + _nbytes(wup.shape, BF16) + _nbytes(wdn.shape, BF16) + 2 * _nbytes(cw.shape, F32)
            + _nbytes((nseg, CONV_PAD + seg, FFN_COLS), F32) + _nbytes((tm, dff), BF16)
            + _nbytes((CONV_PAD, 2 * dff), F32) + 4 * _nbytes((tm, d), F32) + 12 * _nbytes((tm, FFN_COLS), F32))
    o, st = pl.pallas_call(
        functools.partial(_ffn_kernel, nseg=nseg, seg=seg, dff=dff, has_prev=has_prev, has_final=has_final),
        out_shape=(jax.ShapeDtypeStruct((n, d), F32),
                   jax.ShapeDtypeStruct((batch, CONV_W - 1, 2 * dff), F32)),
        grid=grid,
        in_specs=in_specs,
        out_specs=(pl.BlockSpec((tm, d), xmap), pl.BlockSpec((nseg, CONV_W - 1, 2 * dff), stmap)),
        scratch_shapes=[pltpu.VMEM((nseg, CONV_PAD + seg, FFN_COLS), F32),
                        pltpu.VMEM((tm, dff), BF16),
                        pltpu.VMEM((CONV_PAD, 2 * dff), F32)],
        compiler_params=_cparams(2, vmem),
        name="ffn",
    )(*args)
    return o, st


def _layer(x, w, kstack, vstack, *, layer, depth, batch, seq, start, cache_k, cache_v, prev_pool, prev_gla,
           prev_conv):
    d = x.shape[-1]
    bw, kw = d // 2, d // 4
    sb, kstack, vstack, u, gf, vg, h1 = _inproj(x, w["g1"], w["wa"], w["wl"], w["wg"], w["bg"], kstack, vstack,
                                                layer=layer, depth=depth, bw=bw, kw=kw)
    if cache_k is None:
        o_sb = _sb_prompt(sb, batch=batch, seq=seq, bw=bw)
    else:
        o_sb = _sb_sample(sb, cache_k, cache_v, layer=layer, batch=batch, seq=seq, bw=bw)
    o_pool, pool_state = _pool(u, prev_pool, w["w_pool"], w["pool_scale"], batch=batch, seq=seq, start=start)
    o_gla, gla_state = _gla(gf, vg, prev_gla, w["gnorm"], batch=batch, seq=seq, kw=kw, vw=bw)
    x, h2 = _merge(x, h1, w["g2"], o_sb, o_pool, o_gla, w["wgate"], w["wbr"], w["wout"])
    x, conv_state = _ffn(x, h2, prev_conv, w["wup"], w["cw"], w["cb"], w["wdn"],
                         w["final_g"] if layer == depth - 1 else None, batch=batch, seq=seq)
    return x, kstack, vstack, (pool_state, gla_state, conv_state)


def kernel(x_prompt, x_sample, cache_k, cache_v, state_pool, state_gla, state_conv, norm1_g, w_in, w_gla_gate,
           b_gla_gate, gla_norm_g, w_pool, pool_scale, w_branch, w_out, norm2_g, w_up, conv_w, conv_b, w_down,
           final_g):
    bp, tp, d = x_prompt.shape
    bs, ts, _ = x_sample.shape
    depth = w_in.shape[0]
    past = cache_k.shape[2]
    bw, kw = d // 2, d // 4
    rank = w_gla_gate.shape[1]
    n_main = 4 * bw + 2 * kw + 2 * bw
    xp = x_prompt.reshape(bp * tp, d)
    xs = x_sample.reshape(bs * ts, d)
    outs_p, outs_s = [], []
    kp = vp = ks = vs = None
    cache_kt = jnp.transpose(cache_k, (0, 1, 3, 4, 2)).reshape(depth, bs, bw, past)
    cache_vt = jnp.transpose(cache_v, (0, 1, 3, 4, 2)).reshape(depth, bs, bw, past)
    for l in range(depth):
        wl_pad = jnp.zeros((d, LANES), F32).at[:, :rank].set(w_in[l, :, n_main:n_main + rank])
        wg_pad = jnp.zeros((LANES, kw), F32).at[:rank, :].set(w_gla_gate[l])
        w = dict(
            g1=norm1_g[l].reshape(1, d), g2=norm2_g[l].reshape(1, d),
            wa=w_in[l, :, :n_main].astype(BF16), wl=wl_pad.astype(BF16), wg=wg_pad.astype(BF16),
            bg=b_gla_gate[l].reshape(1, kw),
            wgate=w_in[l, :, n_main + rank:].astype(BF16),
            w_pool=w_pool[l].astype(BF16), pool_scale=pool_scale[l].reshape(1, bw),
            gnorm=gla_norm_g[l].reshape(1, -1),
            wbr=w_branch[l].astype(BF16), wout=w_out[l].astype(BF16),
            wup=w_up[l].astype(BF16), cw=conv_w[l], cb=conv_b[l].reshape(1, -1), wdn=w_down[l].astype(BF16),
            final_g=final_g.reshape(1, d),
        )
        xp, kp, vp, st = _layer(xp, w, kp, vp, layer=l, depth=depth, batch=bp, seq=tp, start=0,
                                cache_k=None, cache_v=None, prev_pool=None, prev_gla=None, prev_conv=None)
        outs_p.append(st)
        xs, ks, vs, st = _layer(xs, w, ks, vs, layer=l, depth=depth, batch=bs, seq=ts, start=past,
                                cache_k=cache_kt, cache_v=cache_vt, prev_pool=state_pool[l], prev_gla=state_gla[l],
                                prev_conv=state_conv[l])
        outs_s.append(st)
    y_prompt = xp.reshape(bp, tp, d)
    y_sample = xs.reshape(bs, ts, d)
    stack =lambda outs, i: jnp.stack([o[i] for o in outs])
    heads = bw // SB_HEAD_DIM
    kv_p = (depth, bp, tp, heads, SB_HEAD_DIM)
    kv_s = (depth, bs, ts, heads, SB_HEAD_DIM)
    return (y_prompt, y_sample,
            kp.reshape(kv_p), vp.reshape(kv_p), stack(outs_p, 0), stack(outs_p, 1), stack(outs_p, 2),
            ks.reshape(kv_s), vs.reshape(kv_s), stack(outs_s, 0), stack(outs_s, 1), stack(outs_s, 2))
```

```python
import functools

import jax
import jax.numpy as jnp
from jax import lax
from jax.experimental import pallas as pl
from jax.experimental.pallas import tpu as pltpu

F32 = jnp.float32
BF16 = jnp.bfloat16

NORM_EPS = 1e-6
LOG2E = 1.4426950408889634
CHUNK = 64
SB_HEAD_DIM = 64
POOL_WINDOWS = (2, 4, 8, 16)
POOL_STATE = max(POOL_WINDOWS) - 1
POOL_PAD = 32
GLA_HEADS = 4
GLA_GATE_TAU = 16.0
CONV_W = 3
CONV_PAD = 8
LANES = 128
V7X_SCOPED_VMEM_BYTES = 60000 * 1024

ROW_TILE = 512
GLA_TILE = 1024
SB_BLOCK = 256
SB_PAIRS = 4
SB_DEAD_LOG = -150.0


def _cparams(n_axes, vmem_bytes):
    return pltpu.CompilerParams(
        dimension_semantics=("arbitrary",) * n_axes,
        vmem_limit_bytes=int(min(vmem_bytes, V7X_SCOPED_VMEM_BYTES)),
    )


def _nbytes(shape, dtype):
    n = 1
    for s in shape:
        n *= s
    return n * jnp.dtype(dtype).itemsize


def _const_spec(shape):
    nd = len(shape)
    return pl.BlockSpec(shape, lambda *_: (0,) * nd, pipeline_mode=pl.Buffered(1))


def _rmsnorm(xf, g):
    ms = jnp.mean(xf * xf, axis=-1, keepdims=True)
    return (xf * lax.rsqrt(ms + NORM_EPS)) * g


def _log_sigmoid(x):
    return jnp.minimum(x, 0.0) - jnp.log1p(jnp.exp(-jnp.abs(x)))


def _split_bf16(x):
    hi = x.astype(BF16)
    lo = (x - hi.astype(F32)).astype(BF16)
    return hi, lo


def _dot(a, b):
    return jnp.dot(a, b, preferred_element_type=F32)


def _dot_nt(a, b):
    return lax.dot_general(a, b, (((1,), (1,)), ((), ())), preferred_element_type=F32)


def _dot_tn(a, b):
    return lax.dot_general(a, b, (((0,), (0,)), ((), ())), preferred_element_type=F32)


def _inproj_kernel(*refs, bw, kw, aliased):
    x_ref, g_ref, wa_ref, wl_ref, wg_ref, bg_ref = refs[:6]
    sb_ref, k_ref, v_ref, u_ref, gf_ref, vg_ref = refs[6 + 2 * aliased:]
    h = _rmsnorm(x_ref[...], g_ref[...]).astype(BF16)
    tm = x_ref.shape[0]
    heads = bw // SB_HEAD_DIM

    def proj(c0, c1):
        return _dot(h, wa_ref[:, c0:c1])

    def store_heads(dst_ref, val):
        for hd in range(heads):
            dst_ref[pl.ds(hd, tm, stride=heads), :] = val[:, hd * SB_HEAD_DIM:(hd + 1) * SB_HEAD_DIM]

    sb_ref[:, 0:bw] = (proj(0, bw) * (SB_HEAD_DIM ** -0.5 * LOG2E)).astype(BF16)
    k = proj(bw, 2 * bw)
    store_heads(k_ref, k)
    sb_ref[:, bw:2 * bw] = k.astype(BF16)
    v = proj(2 * bw, 3 * bw)
    store_heads(v_ref, v)
    sb_ref[:, 2 * bw:3 * bw] = v.astype(BF16)
    u_ref[...] = proj(3 * bw, 4 * bw)
    o = 4 * bw
    gf_ref[:, 0:kw] = proj(o, o + kw)
    gf_ref[:, kw:2 * kw] = proj(o + kw, o + 2 * kw)
    vg_ref[...] = proj(o + 2 * kw, o + 2 * kw + bw).astype(BF16)
    gf_ref[:, 3 * kw:3 * kw + bw] = proj(o + 2 * kw + bw, o + 2 * kw + 2 * bw)
    g_lr = _dot(h, wl_ref[...]).astype(BF16)
    pre = _dot(g_lr, wg_ref[...]) + bg_ref[...]
    gf_ref[:, 2 * kw:3 * kw] = _log_sigmoid(pre) * (1.0 / GLA_GATE_TAU)


def _inproj(x, g1, wa, wl, wg, bg, kstack, vstack, *, layer, depth, bw, kw):
    n, d = x.shape
    tm = min(ROW_TILE, n)
    gfw = 3 * kw + bw
    heads = bw // SB_HEAD_DIM
    aliased = kstack is not None
    kv_shape = jax.ShapeDtypeStruct((depth, n * heads, SB_HEAD_DIM), F32)
    out_shape = (
        jax.ShapeDtypeStruct((n, 3 * bw), BF16),
        kv_shape,
        kv_shape,
        jax.ShapeDtypeStruct((n, bw), F32),
        jax.ShapeDtypeStruct((n, gfw), F32),
        jax.ShapeDtypeStruct((n, bw), BF16),
    )
    row = lambda w: pl.BlockSpec((tm, w), lambda i: (i, 0))
    slab = pl.BlockSpec((None, tm * heads, SB_HEAD_DIM), lambda i: (layer, i, 0))
    blocks = 2 * (_nbytes((tm, d), F32) + _nbytes((tm, 3 * bw), BF16) + 2 * _nbytes((tm, heads, LANES), F32)
                  + _nbytes((tm, bw), F32) + _nbytes((tm, gfw), F32) + _nbytes((tm, bw), BF16))
    weights = _nbytes(wa.shape, BF16) + _nbytes(wl.shape, BF16) + _nbytes(wg.shape, BF16)
    temps = 4 * _nbytes((tm, d), F32)
    in_specs = [row(d), _const_spec(g1.shape), _const_spec(wa.shape), _const_spec(wl.shape),
                _const_spec(wg.shape), _const_spec(bg.shape)]
    args = [x, g1, wa, wl, wg, bg]
    if aliased:
        in_specs += [pl.BlockSpec(memory_space=pl.ANY)] * 2
        args += [kstack, vstack]
    return pl.pallas_call(
        functools.partial(_inproj_kernel, bw=bw, kw=kw, aliased=aliased),
        out_shape=out_shape,
        grid=(n // tm,),
        in_specs=in_specs,
        out_specs=(row(3 * bw), slab, slab, row(bw), row(gfw), row(bw)),
        input_output_aliases={6: 1, 7: 2} if aliased else {},
        compiler_params=_cparams(1, blocks + weights + temps),
        name="inproj",
    )(*args)


def _suffix_sum_matrix(n):
    r = lax.broadcasted_iota(jnp.int32, (n, n), 0)
    c = lax.broadcasted_iota(jnp.int32, (n, n), 1)
    return jnp.where(r > c, 1.0, 0.0).astype(BF16)


def _sb_chains(qs, kbs, vbs, u, carries, masks, keys_on_lanes=False):
    n = len(qs)
    sign = jnp.uint32(0x80000000)
    ls_pos, ls_neg, ws, new, pvs = {}, {}, {}, [], []
    for t in range(n + 2):
        if t < n:
            s = _dot(qs[t], kbs[t]) if keys_on_lanes else _dot_nt(qs[t], kbs[t])
            nabs = pltpu.bitcast(pltpu.bitcast(s, jnp.uint32) | sign, F32)
            sp = jnp.log(1.0 + jnp.exp2(nabs)) * LOG2E
            pos = jnp.minimum(s, 0.0) - sp
            neg = pos - s
            ls_pos[t] = pos
            ls_neg[t] = neg if masks[t] is None else jnp.where(masks[t], neg, 0.0)
        i = t - 1
        if 0 <= i < n:
            carry = new[carries[i]] if isinstance(carries[i], int) else carries[i]
            later = _dot(ls_neg[i].astype(BF16), u)
            w = jnp.exp2(ls_pos.pop(i) + later + carry)
            ws[i] = (w if masks[i] is None else jnp.where(masks[i], w, 0.0)).astype(BF16)
            new.append(carry + jnp.sum(ls_neg.pop(i), axis=-1, keepdims=True))
        i = t - 2
        if 0 <= i < n:
            pvs.append(_dot_nt(ws.pop(i), vbs[i]) if keys_on_lanes else _dot(ws.pop(i), vbs[i]))
    return pvs, new


def _head_masks():
    lane = lax.broadcasted_iota(jnp.int32, (1, LANES), 1)
    return lane < SB_HEAD_DIM


def _max_all(carries):
    m = carries[0]
    for c in carries[1:]:
        m = jnp.maximum(m, c)
    return jnp.max(m)


def _sb_prompt_kernel(q_ref, k_ref, v_ref, o_ref, acc_ref, carry_ref, *, blk):
    i = pl.program_id(2)
    npair = q_ref.shape[-1] // LANES
    heads = 2 * npair
    first = _head_masks()
    u = _suffix_sum_matrix(blk)
    r = lax.broadcasted_iota(jnp.int32, (blk, blk), 0)
    c = lax.broadcasted_iota(jnp.int32, (blk, blk), 1)
    causal = c < r
    qs = []
    for p in range(npair):
        q = q_ref[0, :, p * LANES:(p + 1) * LANES]
        qs += [jnp.where(first, q, jnp.zeros_like(q)), jnp.where(first, jnp.zeros_like(q), q)]

    def operands(ref, j):
        off = pl.multiple_of(j * blk, blk)
        blocks = [ref[0, pl.ds(off, blk), p * LANES:(p + 1) * LANES] for p in range(npair)]
        return [blocks[hd // 2] for hd in range(heads)]

    def sweep(j):
        pvs, new = _sb_chains(qs, operands(k_ref, j), operands(v_ref, j), u,
                              [carry_ref[hd] for hd in range(heads)], [None] * heads)
        for hd in range(heads):
            carry_ref[hd] = new[hd]
            acc_ref[hd] += pvs[hd]
        return _max_all(new)

    zero = jnp.zeros((blk, 1), F32)

    @pl.when(i == 0)
    def _():
        pvs, _ = _sb_chains(qs, operands(k_ref, 0), operands(v_ref, 0), u, [zero] * heads, [causal] * heads)
        for hd in range(heads):
            acc_ref[hd] = pvs[hd]

    @pl.when(i > 0)
    def _():
        pvs, new = _sb_chains(qs * 2, operands(k_ref, i) + operands(k_ref, i - 1),
                              operands(v_ref, i) + operands(v_ref, i - 1), u,
                              [zero] * heads + list(range(heads)), [causal] * heads + [None] * heads)
        for hd in range(heads):
            acc_ref[hd] = pvs[hd] + pvs[heads + hd]
            carry_ref[hd] = new[heads + hd]

        def cond(state):
            return jnp.logical_and(state[0] >= 0, state[1] >= SB_DEAD_LOG)

        def body(state):
            return (state[0] - 1, sweep(state[0]))

        lax.while_loop(cond, body, (i - 2, _max_all(new[heads:])))

    for p in range(npair):
        o_ref[0, :, p * LANES:(p + 1) * LANES] = jnp.where(first, acc_ref[2 * p], acc_ref[2 * p + 1]).astype(o_ref.dtype)


def _sb_prompt(sb, *, batch, seq, bw):
    blk = min(SB_BLOCK, seq)
    nq = seq // blk
    width = SB_PAIRS * LANES
    ngrp = bw // width
    sb3 = sb.reshape(batch, seq, 3 * bw)
    vmem = (2 * (2 * _nbytes((blk, width), BF16) + 2 * _nbytes((seq, width), BF16))
            + 4 * SB_PAIRS * _nbytes((blk, LANES), F32) + 2 * SB_PAIRS * 8 * _nbytes((blk, blk), F32))
    out = pl.pallas_call(
        functools.partial(_sb_prompt_kernel, blk=blk),
        out_shape=jax.ShapeDtypeStruct((batch, seq, bw), BF16),
        grid=(batch, ngrp, nq),
        in_specs=[
            pl.BlockSpec((1, blk, width), lambda b, p, i: (b, i, p)),
            pl.BlockSpec((1, seq, width), lambda b, p, i: (b, 0, ngrp + p)),
            pl.BlockSpec((1, seq, width), lambda b, p, i: (b, 0, 2 * ngrp + p)),
        ],
        out_specs=pl.BlockSpec((1, blk, width), lambda b, p, i: (b, i, p)),
        scratch_shapes=[pltpu.VMEM((2 * SB_PAIRS, blk, LANES), F32), pltpu.VMEM((2 * SB_PAIRS, blk, 1), F32)],
        compiler_params=_cparams(3, vmem),
        name="sb_prompt",
    )(sb3, sb3, sb3)
    return out.reshape(batch * seq, bw)


def _sb_sample_kernel(q_ref, kn_ref, vn_ref, kp_hbm, vp_hbm, o_ref, acc_ref, carry_ref, kbuf, vbuf, sem, *,
                      layer, tq, blk, npast):
    b = pl.program_id(0)
    heads = acc_ref.shape[0]
    npair = heads // 2
    first = _head_masks()
    u_new = _suffix_sum_matrix(tq)
    u = _suffix_sum_matrix(blk)
    r = lax.broadcasted_iota(jnp.int32, (tq, tq), 0)
    c = lax.broadcasted_iota(jnp.int32, (tq, tq), 1)
    causal = c < r

    def block_copies(j):
        rows = pl.ds(pl.multiple_of(j * blk, blk), blk)
        return (pltpu.make_async_copy(kp_hbm.at[layer, b, :, rows], kbuf, sem.at[0]),
                pltpu.make_async_copy(vp_hbm.at[layer, b, :, rows], vbuf, sem.at[1]))

    for cp in block_copies(npast - 1):
        cp.start()

    qs = []
    for p in range(npair):
        q = q_ref[0, :, p * LANES:(p + 1) * LANES]
        qs += [jnp.where(first, q, jnp.zeros_like(q)), jnp.where(first, jnp.zeros_like(q), q)]

    pair_lanes = [slice((hd // 2) * LANES, (hd // 2 + 1) * LANES) for hd in range(heads)]
    pvs, carries = _sb_chains(qs, [kn_ref[0, :, ln] for ln in pair_lanes], [vn_ref[0, :, ln] for ln in pair_lanes],
                              u_new, [jnp.zeros((tq, 1), F32)] * heads, [causal] * heads)
    for hd in range(heads):
        acc_ref[hd] = pvs[hd]
        carry_ref[hd] = carries[hd]

    def cond(state):
        return state[1] > 0

    def body(state):
        j = state[0]
        for cp in block_copies(j):
            cp.wait()
        kbs = [kbuf[p * LANES:(p + 1) * LANES, :].astype(BF16) for p in range(npair)]
        vbs = [vbuf[p * LANES:(p + 1) * LANES, :].astype(BF16) for p in range(npair)]
        pvs, new = _sb_chains(qs, [kbs[hd // 2] for hd in range(heads)], [vbs[hd // 2] for hd in range(heads)],
                              u, [carry_ref[hd] for hd in range(heads)], [None] * heads, keys_on_lanes=True)
        for hd in range(heads):
            acc_ref[hd] += pvs[hd]
            carry_ref[hd] = new[hd]
        more = jnp.logical_and(j > 0, _max_all(new) >= SB_DEAD_LOG)

        @pl.when(more)
        def _():
            for cp in block_copies(j - 1):
                cp.start()

        return (j - 1, more.astype(jnp.int32))

    alive = _max_all(carries) >= SB_DEAD_LOG

    @pl.when(jnp.logical_not(alive))
    def _():
        for cp in block_copies(npast - 1):
            cp.wait()

    lax.while_loop(cond, body, (jnp.int32(npast - 1), alive.astype(jnp.int32)))
    for p in range(npair):
        o_ref[0, :, p * LANES:(p + 1) * LANES] = jnp.where(first, acc_ref[2 * p], acc_ref[2 * p + 1]).astype(o_ref.dtype)


def _sb_sample(sb, cache_k, cache_v, *, layer, batch, seq, bw):
    depth, _, _, past = cache_k.shape
    heads = bw // SB_HEAD_DIM
    blk = min(SB_BLOCK, past)
    sb3 = sb.reshape(batch, seq, 3 * bw)
    vmem = (2 * 4 * _nbytes((seq, bw), BF16) + 2 * _nbytes((bw, blk), F32)
            + 2 * heads * _nbytes((seq, LANES), F32) + heads * 8 * _nbytes((seq, blk), F32))
    new = lambda col: pl.BlockSpec((1, seq, bw), lambda b: (b, 0, col))
    out = pl.pallas_call(
        functools.partial(_sb_sample_kernel, layer=layer, tq=seq, blk=blk, npast=past // blk),
        out_shape=jax.ShapeDtypeStruct((batch, seq, bw), BF16),
        grid=(batch,),
        in_specs=[new(0), new(1), new(2), pl.BlockSpec(memory_space=pl.ANY), pl.BlockSpec(memory_space=pl.ANY)],
        out_specs=new(0),
        scratch_shapes=[pltpu.VMEM((heads, seq, LANES), F32),
                        pltpu.VMEM((heads, seq, 1), F32),
                        pltpu.VMEM((bw, blk), F32),
                        pltpu.VMEM((bw, blk), F32),
                        pltpu.SemaphoreType.DMA((2,))],
        compiler_params=_cparams(1, vmem),
        name="sb_sample",
    )(sb3, sb3, sb3, cache_k, cache_v)
    return out.reshape(batch * seq, bw)


def _pool_kernel(*refs, tq, gdim, start, has_prev):
    if has_prev:
        u_ref, prev_ref, w_ref, sc_ref, o_ref, st_ref, ext_ref, lvl_ref = refs
    else:
        u_ref, w_ref, sc_ref, o_ref, st_ref, ext_ref, lvl_ref = refs
    t = pl.program_id(1)

    @pl.when(t == 0)
    def _():
        ext_ref[0:POOL_PAD, :] = jnp.zeros((POOL_PAD, ext_ref.shape[1]), F32)
        if has_prev:
            ext_ref[POOL_PAD - POOL_STATE:POOL_PAD, :] = prev_ref[0]

    @pl.when(t > 0)
    def _():
        ext_ref[0:POOL_PAD, :] = ext_ref[tq:tq + POOL_PAD, :]

    ext_ref[POOL_PAD:POOL_PAD + tq, :] = u_ref[0]
    pos = start + t * tq + lax.broadcasted_iota(jnp.int32, (tq, 1), 0)
    nrow = POOL_PAD + tq
    for g, w in enumerate(POOL_WINDOWS):
        cols = slice(g * gdim, (g + 1) * gdim)
        tok = ext_ref[POOL_PAD:nrow, cols]
        def level(slot, r0, r1):
            return ext_ref[r0:r1, cols] if slot is None else lvl_ref[slot, r0:r1, :]

        slot, lo, m = None, 0, 1
        while 2 * m < w:
            lo += 8
            nxt = 0 if slot != 0 else 1
            lvl_ref[nxt, lo:nrow, :] = level(slot, lo, nrow) + level(slot, lo - m, nrow - m)
            slot, m = nxt, 2 * m
        s = level(slot, POOL_PAD, nrow) + level(slot, POOL_PAD - m, nrow - m)
        cnt = jnp.minimum(pos + 1, w).astype(F32)
        pooled = (s / cnt - tok).astype(BF16)
        o_ref[0, :, cols] = (_dot(pooled, w_ref[g]) * sc_ref[:, cols]).astype(o_ref.dtype)

    @pl.when(t == pl.num_programs(1) - 1)
    def _():
        st_ref[0] = ext_ref[POOL_PAD + tq - POOL_STATE:POOL_PAD + tq, :]


def _pool(u, prev, w_pool, scale, *, batch, seq, start):
    bw = u.shape[-1]
    tq = min(ROW_TILE, seq)
    gdim = bw // len(POOL_WINDOWS)
    has_prev = prev is not None
    u3 = u.reshape(batch, seq, bw)
    in_specs = [pl.BlockSpec((1, tq, bw), lambda b, t: (b, t, 0))]
    args = [u3]
    if has_prev:
        in_specs.append(pl.BlockSpec((1, POOL_STATE, bw), lambda b, t: (b, 0, 0)))
        args.append(prev)
    in_specs += [_const_spec(w_pool.shape), _const_spec(scale.shape)]
    args += [w_pool, scale]
    vmem = (2 * (_nbytes((tq, bw), F32) + _nbytes((tq, bw), BF16) + 2 * _nbytes((POOL_PAD, bw), F32))
            + _nbytes((POOL_PAD + tq, bw), F32) + 2 * _nbytes((POOL_PAD + tq, gdim), F32)
            + _nbytes(w_pool.shape, BF16) + 6 * _nbytes((tq, bw), F32))
    o, st = pl.pallas_call(
        functools.partial(_pool_kernel, tq=tq, gdim=gdim, start=start, has_prev=has_prev),
        out_shape=(jax.ShapeDtypeStruct((batch, seq, bw), BF16),
                   jax.ShapeDtypeStruct((batch, POOL_STATE, bw), F32)),
        grid=(batch, seq // tq),
        in_specs=in_specs,
        out_specs=(pl.BlockSpec((1, tq, bw), lambda b, t: (b, t, 0)),
                   pl.BlockSpec((1, POOL_STATE, bw), lambda b, t: (b, 0, 0))),
        scratch_shapes=[pltpu.VMEM((POOL_PAD + tq, bw), F32), pltpu.VMEM((2, POOL_PAD + tq, gdim), F32)],
        compiler_params=_cparams(2, vmem),
        name="pool",
    )(*args)
    return o.reshape(batch * seq, bw), st


def _gla_kernel(*refs, tt, kw, vw, has_state):
    if has_state:
        gf_ref, vg_ref, s0_ref, gn_ref, o_ref, s_out_ref, s_ref, og_ref = refs
    else:
        gf_ref, vg_ref, gn_ref, o_ref, s_out_ref, s_ref, og_ref = refs
    t = pl.program_id(1)
    dk = kw // GLA_HEADS
    dv = vw // GLA_HEADS
    nh = GLA_HEADS

    @pl.when(t == 0)
    def _():
        if has_state:
            s_ref[...] = s0_ref[0]
        else:
            s_ref[...] = jnp.zeros(s_ref.shape, F32)

    r = lax.broadcasted_iota(jnp.int32, (CHUNK, CHUNK), 0)
    c = lax.broadcasted_iota(jnp.int32, (CHUNK, CHUNK), 1)
    tril = jnp.where(r >= c, 1.0, 0.0).astype(BF16)
    lane_head = lax.broadcasted_iota(jnp.int32, (CHUNK, kw), 1) // dk
    head_lanes = [lane_head == h for h in range(nh)]
    qi = lax.broadcasted_iota(jnp.int32, (nh * CHUNK, CHUNK), 0) % CHUNK
    kj = lax.broadcasted_iota(jnp.int32, (nh * CHUNK, CHUNK), 1)
    causal = qi >= kj

    def stack_heads(x):
        return jnp.concatenate([jnp.where(m, x, 0.0) for m in head_lanes], axis=0).astype(BF16)

    def diag(x, h, width):
        rows = x.shape[0] // nh
        return x[h * rows:(h + 1) * rows, h * width:(h + 1) * width]

    def prepare(ci):
        rows = slice(ci * CHUNK, (ci + 1) * CHUNK)
        q = gf_ref[0, rows, 0:kw] * (dk ** -0.5)
        k = gf_ref[0, rows, kw:2 * kw]
        hi, lo = _split_bf16(gf_ref[0, rows, 2 * kw:3 * kw])
        b = _dot(tril, hi) + _dot(tril, lo)
        bref = b[CHUNK // 2:CHUNK // 2 + 1, :]
        blast = b[CHUNK - 1:CHUNK, :]
        blast_col = jnp.broadcast_to(blast, (dv, kw)).T
        return dict(qe=stack_heads(q * jnp.exp(b - bref)), qb=stack_heads(q * jnp.exp(b)),
                    ke=(k * jnp.exp(bref - b)).astype(BF16), kl=(k * jnp.exp(blast - b)).astype(BF16),
                    decay=jnp.exp(blast_col))

    def scores(ci, p):
        v = vg_ref[0, ci * CHUNK:(ci + 1) * CHUNK, :]
        att = jnp.where(causal, _dot_nt(p.pop("qe"), p.pop("ke")), 0.0).astype(BF16)
        upd = _dot_tn(p.pop("kl"), v)
        p.update(att=att, upd=jnp.concatenate([diag(upd, h, dv) for h in range(nh)], axis=0))

    def finish(ci, p):
        rows = slice(ci * CHUNK, (ci + 1) * CHUNK)
        s_old = s_ref[...]
        inter = _dot(p["qb"], s_old.astype(BF16))
        for h in range(nh):
            hrows = slice(h * CHUNK, (h + 1) * CHUNK)
            intra = _dot(p["att"][hrows, :], vg_ref[0, rows, h * dv:(h + 1) * dv])
            og_ref[rows, h * dv:(h + 1) * dv] = intra + inter[hrows, :]
        s_ref[...] = s_old * p["decay"] + p["upd"]

    nchunk = tt // CHUNK
    work = {}
    for step in range(nchunk + 2):
        if step < nchunk:
            work[step] = prepare(step)
        if 0 <= step - 1 < nchunk:
            scores(step - 1, work[step - 1])
        if 0 <= step - 2 < nchunk:
            finish(step - 2, work.pop(step - 2))

    gn = gn_ref[...]
    for h in range(GLA_HEADS):
        cols = slice(h * dv, (h + 1) * dv)
        y = _rmsnorm(og_ref[:, cols], gn)
        rg = gf_ref[0, :, 3 * kw + h * dv:3 * kw + (h + 1) * dv]
        o_ref[0, :, cols] = (y * (rg / (1.0 + jnp.exp(-rg)))).astype(o_ref.dtype)

    @pl.when(t == pl.num_programs(1) - 1)
    def _():
        s_out_ref[0] = s_ref[...]


def _gla(gf, vg, s0, gnorm, *, batch, seq, kw, vw):
    tt = min(GLA_TILE, seq)
    dk, dv = kw // GLA_HEADS, vw // GLA_HEADS
    gfw = gf.shape[-1]
    has_state = s0 is not None
    in_specs = [pl.BlockSpec((1, tt, gfw), lambda b, t: (b, t, 0)),
                pl.BlockSpec((1, tt, vw), lambda b, t: (b, t, 0))]
    args = [gf.reshape(batch, seq, gfw), vg.reshape(batch, seq, vw)]
    state_spec = pl.BlockSpec((1, kw, dv), lambda b, t: (b, 0, 0))
    if has_state:
        in_specs.append(state_spec)
        args.append(s0.reshape(batch, kw, dv))
    in_specs.append(_const_spec(gnorm.shape))
    args.append(gnorm)
    vmem = (2 * (_nbytes((tt, gfw), F32) + 2 * _nbytes((tt, vw), BF16) + 2 * _nbytes((kw, dv), F32))
            + _nbytes((kw, dv), F32) + _nbytes((tt, vw), F32) + 16 * _nbytes((GLA_HEADS * CHUNK, vw), F32))
    o, st = pl.pallas_call(
        functools.partial(_gla_kernel, tt=tt, kw=kw, vw=vw, has_state=has_state),
        out_shape=(jax.ShapeDtypeStruct((batch, seq, vw), BF16),
                   jax.ShapeDtypeStruct((batch, kw, dv), F32)),
        grid=(batch, seq // tt),
        in_specs=in_specs,
        out_specs=(pl.BlockSpec((1, tt, vw), lambda b, t: (b, t, 0)), state_spec),
        scratch_shapes=[pltpu.VMEM((kw, dv), F32), pltpu.VMEM((tt, vw), F32)],
        compiler_params=_cparams(2, vmem),
        name="gla",
    )(*args)
    return o.reshape(batch * seq, vw), st.reshape(batch, GLA_HEADS, dk, dv)


def _merge_kernel(x_ref, g_ref, osb_ref, opool_ref, ogla_ref, wgate_ref, wbr_ref, wout_ref, o_ref, *, d):
    xf = x_ref[...]
    h = _rmsnorm(xf, g_ref[...]).astype(BF16)
    m = None
    for n, br_ref in enumerate((osb_ref, opool_ref, ogla_ref)):
        logits = _dot(h, wgate_ref[:, n * d:(n + 1) * d])
        gate = 1.0 / (1.0 + jnp.exp(-logits))
        term = gate * _dot(br_ref[...], wbr_ref[n])
        m = term if m is None else m + term
    o_ref[...] = xf + _dot(m.astype(BF16), wout_ref[...])


def _merge(x, g1, o_sb, o_pool, o_gla, wgate, wbr, wout):
    n, d = x.shape
    bw = o_sb.shape[-1]
    tm = min(ROW_TILE, n)
    row = lambda w: pl.BlockSpec((tm, w), lambda i: (i, 0))
    vmem = (2 * (2 * _nbytes((tm, d), F32) + 3 * _nbytes((tm, bw), BF16))
            + _nbytes(wgate.shape, BF16) + _nbytes(wbr.shape, BF16) + _nbytes(wout.shape, BF16)
            + 6 * _nbytes((tm, d), F32))
    return pl.pallas_call(
        functools.partial(_merge_kernel, d=d),
        out_shape=jax.ShapeDtypeStruct((n, d), F32),
        grid=(n // tm,),
        in_specs=[row(d), _const_spec(g1.shape), row(bw), row(bw), row(bw),
                  _const_spec(wgate.shape), _const_spec(wbr.shape), _const_spec(wout.shape)],
        out_specs=row(d),
        compiler_params=_cparams(1, vmem),
        name="merge",
    )(x, g1, o_sb, o_pool, o_gla, wgate, wbr, wout)


FFN_COLS = 256


def _gelu_tanh(x):
    c = 0.7978845608028654
    return 0.5 * x * (1.0 + jnp.tanh(c * (x + 0.044715 * (x * x * x))))


def _ffn_kernel(*refs, nseg, seg, dff, has_prev, has_final):
    refs = list(refs)
    x_ref, g_ref = refs[:2]
    prev_ref = refs[2] if has_prev else None
    wup_ref, cw_ref, cb_ref, wdn_ref = refs[2 + has_prev:6 + has_prev]
    fg_ref = refs[6 + has_prev] if has_final else None
    o_ref, st_ref, ext_ref, gact_ref, tail_ref = refs[6 + has_prev + has_final:]
    t = pl.program_id(1)
    xf = x_ref[...]
    h = _rmsnorm(xf, g_ref[...]).astype(BF16)
    pad0 = CONV_PAD - (CONV_W - 1)

    if not has_prev:
        @pl.when(t == 0)
        def _():
            tail_ref[...] = jnp.zeros(tail_ref.shape, F32)

    def conv_half(c0):
        cols = slice(c0, c0 + FFN_COLS)
        up = _dot(h, wup_ref[:, cols])
        outs = []
        for s in range(nseg):
            if has_prev:
                ext_ref[s, pad0:CONV_PAD, :] = prev_ref[s, :, cols]
            else:
                ext_ref[s, 0:CONV_PAD, :] = tail_ref[:, cols]
            ext_ref[s, CONV_PAD:CONV_PAD + seg, :] = up[s * seg:(s + 1) * seg, :]
            acc = cb_ref[:, cols] + ext_ref[s, pad0:pad0 + seg, :] * cw_ref[0:1, cols]
            for i in range(1, CONV_W):
                acc = acc + ext_ref[s, pad0 + i:pad0 + i + seg, :] * cw_ref[i:i + 1, cols]
            outs.append(acc)
            if has_prev:
                st_ref[s, :, cols] = ext_ref[s, CONV_PAD + seg - (CONV_W - 1):CONV_PAD + seg, :]
            else:
                tail_ref[:, cols] = ext_ref[s, seg:seg + CONV_PAD, :]
        return outs[0] if nseg == 1 else jnp.concatenate(outs, axis=0)

    for c0 in range(0, dff, FFN_COLS):
        a = conv_half(c0)
        b = conv_half(dff + c0)
        gact_ref[:, c0:c0 + FFN_COLS] = (_gelu_tanh(a) * b).astype(BF16)

    y = xf + _dot(gact_ref[...], wdn_ref[...])
    o_ref[...] = _rmsnorm(y, fg_ref[...]) if has_final else y

    if not has_prev:
        @pl.when(t == pl.num_programs(1) - 1)
        def _():
            st_ref[0] = tail_ref[CONV_PAD - (CONV_W - 1):CONV_PAD, :]


def _ffn(x, g2, prev, wup, cw, cb, wdn, final_g, *, batch, seq):
    n, d = x.shape
    dff = wdn.shape[0]
    has_prev = prev is not None
    has_final = final_g is not None
    if has_prev:
        nseg = max(1, min(ROW_TILE // seq, batch))
        seg = seq
        grid = (batch // nseg, 1)
        xmap = lambda i, t: (i, 0)
        stmap = lambda i, t: (i, 0, 0)
    else:
        nseg = 1
        seg = min(ROW_TILE, seq)
        grid = (batch, seq // seg)
        nt = seq // seg
        xmap = lambda b, t: (b * nt + t, 0)
        stmap = lambda b, t: (b, 0, 0)
    tm = nseg * seg
    in_specs = [pl.BlockSpec((tm, d), xmap), _const_spec(g2.shape)]
    args = [x, g2]
    if has_prev:
        in_specs.append(pl.BlockSpec((nseg, CONV_W - 1, 2 * dff), stmap))
        args.append(prev)
    in_specs += [_const_spec(wup.shape), _const_spec(cw.shape), _const_spec(cb.shape), _const_spec(wdn.shape)]
    args += [wup, cw, cb, wdn]
    if has_final:
        in_specs.append(_const_spec(final_g.shape))
        args.append(final_g)
    vmem = (2 * (2 * _nbytes((tm, d), F32) + 2 * _nbytes((nseg, CONV_W - 1, 2 * dff), F32))
            + _nbytes(wup.shape, BF16) + _nbytes(wdn.shape, BF16) + 2 * _nbytes(cw.shape, F32)
            + _nbytes((nseg, CONV_PAD + seg, FFN_COLS), F32) + _nbytes((tm, dff), BF16)
            + _nbytes((CONV_PAD, 2 * dff), F32) + 4 * _nbytes((tm, d), F32) + 12 * _nbytes((tm, FFN_COLS), F32))
    o, st = pl.pallas_call(
        functools.partial(_ffn_kernel, nseg=nseg, seg=seg, dff=dff, has_prev=has_prev, has_final=has_final),
        out_shape=(jax.ShapeDtypeStruct((n, d), F32),
                   jax.ShapeDtypeStruct((batch, CONV_W - 1, 2 * dff), F32)),
        grid=grid,
        in_specs=in_specs,
        out_specs=(pl.BlockSpec((tm, d), xmap), pl.BlockSpec((nseg, CONV_W - 1, 2 * dff), stmap)),
        scratch_shapes=[pltpu.VMEM((nseg, CONV_PAD + seg, FFN_COLS), F32),
                        pltpu.VMEM((tm, dff), BF16),
                        pltpu.VMEM((CONV_PAD, 2 * dff), F32)],
        compiler_params=_cparams(2, vmem),
        name="ffn",
    )(*args)
    return o, st


def _layer(x, w, kstack, vstack, *, layer, depth, batch, seq, start, cache_k, cache_v, prev_pool, prev_gla,
           prev_conv):
    d = x.shape[-1]
    bw, kw = d // 2, d // 4
    sb, kstack, vstack, u, gf, vg = _inproj(x, w["g1"], w["wa"], w["wl"], w["wg"], w["bg"], kstack, vstack,
                                            layer=layer, depth=depth, bw=bw, kw=kw)
    if cache_k is None:
        o_sb = _sb_prompt(sb, batch=batch, seq=seq, bw=bw)
    else:
        o_sb = _sb_sample(sb, cache_k, cache_v, layer=layer, batch=batch, seq=seq, bw=bw)
    o_pool, pool_state = _pool(u, prev_pool, w["w_pool"], w["pool_scale"], batch=batch, seq=seq, start=start)
    o_gla, gla_state = _gla(gf, vg, prev_gla, w["gnorm"], batch=batch, seq=seq, kw=kw, vw=bw)
    x = _merge(x, w["g1"], o_sb, o_pool, o_gla, w["wgate"], w["wbr"], w["wout"])
    x, conv_state = _ffn(x, w["g2"], prev_conv, w["wup"], w["cw"], w["cb"], w["wdn"],
                         w["final_g"] if layer == depth - 1 else None, batch=batch, seq=seq)
    return x, kstack, vstack, (pool_state, gla_state, conv_state)


def kernel(x_prompt, x_sample, cache_k, cache_v, state_pool, state_gla, state_conv, norm1_g, w_in, w_gla_gate,
           b_gla_gate, gla_norm_g, w_pool, pool_scale, w_branch, w_out, norm2_g, w_up, conv_w, conv_b, w_down,
           final_g):
    bp, tp, d = x_prompt.shape
    bs, ts, _ = x_sample.shape
    depth = w_in.shape[0]
    past = cache_k.shape[2]
    bw, kw = d // 2, d // 4
    rank = w_gla_gate.shape[1]
    n_main = 4 * bw + 2 * kw + 2 * bw
    xp = x_prompt.reshape(bp * tp, d)
    xs = x_sample.reshape(bs * ts, d)
    outs_p, outs_s = [], []
    kp = vp = ks = vs = None
    cache_kt = jnp.transpose(cache_k, (0, 1, 3, 4, 2)).reshape(depth, bs, bw, past)
    cache_vt = jnp.transpose(cache_v, (0, 1, 3, 4, 2)).reshape(depth, bs, bw, past)
    for l in range(depth):
        wl_pad = jnp.zeros((d, LANES), F32).at[:, :rank].set(w_in[l, :, n_main:n_main + rank])
        wg_pad = jnp.zeros((LANES, kw), F32).at[:rank, :].set(w_gla_gate[l])
        w = dict(
            g1=norm1_g[l].reshape(1, d), g2=norm2_g[l].reshape(1, d),
            wa=w_in[l, :, :n_main].astype(BF16), wl=wl_pad.astype(BF16), wg=wg_pad.astype(BF16),
            bg=b_gla_gate[l].reshape(1, kw),
            wgate=w_in[l, :, n_main + rank:].astype(BF16),
            w_pool=w_pool[l].astype(BF16), pool_scale=pool_scale[l].reshape(1, bw),
            gnorm=gla_norm_g[l].reshape(1, -1),
            wbr=w_branch[l].astype(BF16), wout=w_out[l].astype(BF16),
            wup=w_up[l].astype(BF16), cw=conv_w[l], cb=conv_b[l].reshape(1, -1), wdn=w_down[l].astype(BF16),
            final_g=final_g.reshape(1, d),
        )
        xp, kp, vp, st = _layer(xp, w, kp, vp, layer=l, depth=depth, batch=bp, seq=tp, start=0,
                                cache_k=None, cache_v=None, prev_pool=None, prev_gla=None, prev_conv=None)
        outs_p.append(st)
        xs, ks, vs, st = _layer(xs, w, ks, vs, layer=l, depth=depth, batch=bs, seq=ts, start=past,
                                cache_k=cache_kt, cache_v=cache_vt, prev_pool=state_pool[l], prev_gla=state_gla[l],
                                prev_conv=state_conv[l])
        outs_s.append(st)
    y_prompt = xp.reshape(bp, tp, d)
    y_sample = xs.reshape(bs, ts, d)
    stack =lambda outs, i: jnp.stack([o[i] for o in outs])
    heads = bw // SB_HEAD_DIM
    kv_p = (depth, bp, tp, heads, SB_HEAD_DIM)
    kv_s = (depth, bs, ts, heads, SB_HEAD_DIM)
    return (y_prompt, y_sample,
            kp.reshape(kv_p), vp.reshape(kv_p), stack(outs_p, 0), stack(outs_p, 1), stack(outs_p, 2),
            ks.reshape(kv_s), vs.reshape(kv_s), stack(outs_s, 0), stack(outs_s, 1), stack(outs_s, 2))
```
